```python
import math
import jax, jax.numpy as jnp
from jax import lax
import numpy as np

D_MODEL = 2048
BATCH = 4
SEQ = 8192
DEPTH = 1

CHUNK = 64
Q_BLOCK = 128
POOL_WINDOWS = (2, 4, 8, 16)
POOL_GROUPS = 4
POOL_WIDTH = D_MODEL // 2
POOL_GROUP_DIM = POOL_WIDTH // POOL_GROUPS
N_HEADS = 8
HEAD_DIM = D_MODEL // 32
V_DIM = 2 * HEAD_DIM
QK_WIDTH = N_HEADS * 2 * HEAD_DIM
ATTN_WIDTH = N_HEADS * V_DIM
N_BRANCHES = 2
IN_COLS = POOL_WIDTH + 2 * QK_WIDTH + ATTN_WIDTH + N_BRANCHES * D_MODEL
NUM_BUCKETS = 32
MAX_DISTANCE = 128
PEER_HEADS = 8
N_KEYS = 128
N_EXPERTS = N_KEYS * N_KEYS
PEER_KEY_DIM = 256
PEER_HALF = PEER_KEY_DIM // 2
PEER_TOPK = 16
PEER_TOKEN_BLOCK = 128
PLE_DIM = 256
ALPHA = (2 * DEPTH) ** 0.25
BETA = (8 * DEPTH) ** -0.25
LN_EPS = 1e-5
MASK_VALUE = -1e30

kernel_name = "hybrid_pool_diffattn_peer_deepnorm"


def layer_norm(x, g, b):
    xf = x.astype(jnp.float32)
    mu = jnp.mean(xf, axis=-1, keepdims=True)
    var = jnp.mean(jnp.square(xf - mu), axis=-1, keepdims=True)
    y = (xf - mu) * lax.rsqrt(var + LN_EPS) * g.astype(jnp.float32) + b.astype(jnp.float32)
    return y.astype(x.dtype)


def rms_norm(x, g):
    xf = x.astype(jnp.float32)
    y = xf * lax.rsqrt(jnp.mean(jnp.square(xf), axis=-1, keepdims=True) + LN_EPS)
    return (y * g.astype(jnp.float32)).astype(x.dtype)


def t5_bucket(rel):
    nb = NUM_BUCKETS // 2
    ret = (rel > 0).astype(jnp.int32) * nb
    n = jnp.abs(rel)
    max_exact = nb // 2
    is_small = n < max_exact
    nf = jnp.maximum(n, 1).astype(jnp.float32)
    large = max_exact + (jnp.log(nf / max_exact) / math.log(MAX_DISTANCE / max_exact)
                         * (nb - max_exact)).astype(jnp.int32)
    large = jnp.minimum(large, nb - 1)
    return ret + jnp.where(is_small, n, large)


def causal_pool_mixer(xp, w_grp, scale):
    B, S, _ = xp.shape
    xf = xp.astype(jnp.float32)
    cs = jnp.pad(jnp.cumsum(xf, axis=1), ((0, 0), (1, 0), (0, 0)))
    t = jnp.arange(S)
    means = []
    for gi, w in enumerate(POOL_WINDOWS):
        lo, hi = gi * POOL_GROUP_DIM, (gi + 1) * POOL_GROUP_DIM
        start = jnp.maximum(t + 1 - w, 0)
        win = jnp.take(cs[..., lo:hi], t + 1, axis=1) - jnp.take(cs[..., lo:hi], start, axis=1)
        cnt = (t + 1 - start).astype(jnp.float32)
        means.append(win / cnt[None, :, None])
    mixed = (jnp.concatenate(means, axis=-1) - xf).astype(xp.dtype)
    grp = mixed.reshape(B, S, POOL_GROUPS, POOL_GROUP_DIM)
    y = jnp.einsum('bsgc,gcd->bsgd', grp, w_grp).reshape(B, S, POOL_WIDTH)
    return y * scale


def diff_attention(q1, q2, k1, k2, v, lam, bias_table):
    B, S = q1.shape[0], q1.shape[1]
    nblk = S // Q_BLOCK
    scale = HEAD_DIM ** -0.5
    kpos = jnp.arange(S)
    kchunk = kpos // CHUNK

    def to_blocks(q):
        return q.reshape(B, nblk, Q_BLOCK, N_HEADS, HEAD_DIM).transpose(1, 0, 2, 3, 4)

    def one_block(args):
        qa, qb, bi = args
        qpos = bi * Q_BLOCK + jnp.arange(Q_BLOCK)
        rel = kpos[None, :] - qpos[:, None]
        bias = jnp.transpose(bias_table[t5_bucket(rel)].astype(jnp.float32), (2, 0, 1))
        mask = kchunk[None, :] <= (qpos // CHUNK)[:, None]

        def probs(q, k):
            s = jnp.einsum('bqhd,bkhd->bhqk', q, k).astype(jnp.float32) * scale + bias
            s = jnp.where(mask, s, MASK_VALUE)
            return jax.nn.softmax(s, axis=-1)

        a = probs(qa, k1) - lam * probs(qb, k2)
        return jnp.einsum('bhqk,bkhe->bqhe', a.astype(v.dtype), v)

    out = lax.map(one_block, (to_blocks(q1), to_blocks(q2), jnp.arange(nblk)))
    return out.transpose(1, 0, 2, 3, 4).reshape(B, S, N_HEADS, V_DIM)


def peer_layer(x, wq, keys1, keys2, u_tab, v_tab):
    B, S, D = x.shape
    q = jnp.einsum('bsd,de->bse', x, wq).reshape(B, S, PEER_HEADS, 2, PEER_HALF)
    s1 = jnp.einsum('bshc,kc->bshk', q[..., 0, :], keys1).astype(jnp.float32)
    s2 = jnp.einsum('bshc,kc->bshk', q[..., 1, :], keys2).astype(jnp.float32)
    v1, i1 = lax.top_k(s1, PEER_TOPK)
    v2, i2 = lax.top_k(s2, PEER_TOPK)
    cand = (v1[..., :, None] + v2[..., None, :]).reshape(B, S, PEER_HEADS, PEER_TOPK * PEER_TOPK)
    cidx = (i1[..., :, None] * N_KEYS + i2[..., None, :]).reshape(B, S, PEER_HEADS, PEER_TOPK * PEER_TOPK)
    vals, pos = lax.top_k(cand, PEER_TOPK)
    ids = jnp.take_along_axis(cidx, pos, axis=-1)
    gates = jax.nn.softmax(vals, axis=-1)
    T = B * S
    nb = T // PEER_TOKEN_BLOCK
    xt = x.reshape(nb, PEER_TOKEN_BLOCK, D)
    ids_t = ids.reshape(nb, PEER_TOKEN_BLOCK, PEER_HEADS * PEER_TOPK)
    g_t = gates.reshape(nb, PEER_TOKEN_BLOCK, PEER_HEADS * PEER_TOPK)

    def one_block(args):
        xb, ib, gb = args
        ug = jnp.take(u_tab, ib, axis=0)
        vg = jnp.take(v_tab, ib, axis=0)
        h = jnp.einsum('td,tkd->tk', xb, ug)
        a = (jax.nn.gelu(h.astype(jnp.float32), approximate=False) * gb).astype(xb.dtype)
        return jnp.einsum('tk,tkd->td', a, vg)

    out = lax.map(one_block, (xt, ids_t, g_t))
    return out.reshape(B, S, D)


def setup_inputs(seed: int = 0) -> dict:
    key = jax.random.key(seed)
    ks = jax.random.split(key, 32)
    f32 = jnp.float32
    nrm = lambda k, shape, s: jax.random.normal(k, shape, f32) * s
    L, D = DEPTH, D_MODEL
    return {
        "x": nrm(ks[0], (BATCH, SEQ, D), 1.0),
        "p": nrm(ks[1], (DEPTH, BATCH, SEQ, PLE_DIM), 1.0),
        "ln_in_g": 1.0 + nrm(ks[2], (D,), 0.02),
        "ln_in_b": nrm(ks[3], (D,), 0.02),
        "w_in": nrm(ks[4], (L, D, IN_COLS), D ** -0.5),
        "pool_grp_w": nrm(ks[5], (L, POOL_GROUPS, POOL_GROUP_DIM, POOL_GROUP_DIM), POOL_GROUP_DIM ** -0.5),
        "pool_scale": 1.0 + nrm(ks[6], (L, POOL_WIDTH), 0.1),
        "lambda_q1": nrm(ks[7], (L, HEAD_DIM), 0.1),
        "lambda_k1": nrm(ks[8], (L, HEAD_DIM), 0.1),
        "lambda_q2": nrm(ks[9], (L, HEAD_DIM), 0.1),
        "lambda_k2": nrm(ks[10], (L, HEAD_DIM), 0.1),
        "subln_g": 1.0 + nrm(ks[11], (L, V_DIM), 0.02),
        "w_branch_pool": nrm(ks[12], (L, POOL_WIDTH, D), POOL_WIDTH ** -0.5 * BETA),
        "w_branch_attn": nrm(ks[13], (L, ATTN_WIDTH, D), ATTN_WIDTH ** -0.5 * BETA),
        "gate_b": nrm(ks[14], (L, N_BRANCHES, D), 0.1),
        "w_out": nrm(ks[15], (L, D, D), D ** -0.5 * BETA),
        "ln1_g": 1.0 + nrm(ks[16], (L, D), 0.02),
        "ln1_b": nrm(ks[17], (L, D), 0.02),
        "peer_wq": nrm(ks[18], (L, D, PEER_HEADS * PEER_KEY_DIM), D ** -0.5),
        "peer_keys1": nrm(ks[19], (L, N_KEYS, PEER_HALF), PEER_HALF ** -0.5),
        "peer_keys2": nrm(ks[20], (L, N_KEYS, PEER_HALF), PEER_HALF ** -0.5),
        "peer_u": nrm(ks[21], (L, N_EXPERTS, D), D ** -0.5),
        "peer_v": nrm(ks[22], (L, N_EXPERTS, D), BETA),
        "ple_w": nrm(ks[23], (L, PLE_DIM, D), PLE_DIM ** -0.5 * BETA),
        "ple_gate_w": nrm(ks[24], (L, D, D), D ** -0.5),
        "ln2_g": 1.0 + nrm(ks[25], (L, D), 0.02),
        "ln2_b": nrm(ks[26], (L, D), 0.02),
        "rel_bias": nrm(ks[27], (NUM_BUCKETS, N_HEADS), 0.5),
    }


def reference(x, p, ln_in_g, ln_in_b, w_in, pool_grp_w, pool_scale,
              lambda_q1, lambda_k1, lambda_q2, lambda_k2, subln_g,
              w_branch_pool, w_branch_attn, gate_b, w_out, ln1_g, ln1_b,
              peer_wq, peer_keys1, peer_keys2, peer_u, peer_v,
              ple_w, ple_gate_w, ln2_g, ln2_b, rel_bias):
    B, S, D = x.shape
    c0 = POOL_WIDTH
    c1 = c0 + QK_WIDTH
    c2 = c1 + QK_WIDTH
    c3 = c2 + ATTN_WIDTH
    x = layer_norm(x, ln_in_g, ln_in_b)
    for i in range(DEPTH):
        h = jnp.einsum('bsd,de->bse', x, w_in[i])
        xp = h[..., :c0]
        q = h[..., c0:c1].reshape(B, S, N_HEADS, 2, HEAD_DIM)
        k = h[..., c1:c2].reshape(B, S, N_HEADS, 2, HEAD_DIM)
        v = h[..., c2:c3].reshape(B, S, N_HEADS, V_DIM)
        g_pre = h[..., c3:].reshape(B, S, N_BRANCHES, D)

        yp = causal_pool_mixer(xp, pool_grp_w[i], pool_scale[i])

        lam_init = 0.8 - 0.6 * math.exp(-0.3 * i)
        lam = (jnp.exp(jnp.sum(lambda_q1[i].astype(jnp.float32) * lambda_k1[i].astype(jnp.float32)))
               - jnp.exp(jnp.sum(lambda_q2[i].astype(jnp.float32) * lambda_k2[i].astype(jnp.float32)))
               + lam_init)
        att = diff_attention(q[..., 0, :], q[..., 1, :], k[..., 0, :], k[..., 1, :], v, lam, rel_bias)
        att = rms_norm(att, subln_g[i]) * (1.0 - lam_init)
        ya = att.reshape(B, S, ATTN_WIDTH)

        gates = jax.nn.sigmoid(g_pre + gate_b[i])
        merged = (gates[..., 0, :] * jnp.einsum('bsc,cd->bsd', yp, w_branch_pool[i])
                  + gates[..., 1, :] * jnp.einsum('bsc,cd->bsd', ya, w_branch_attn[i]))
        y = jnp.einsum('bsd,de->bse', merged, w_out[i])
        x = layer_norm(ALPHA * x + y, ln1_g[i], ln1_b[i])

        f = peer_layer(x, peer_wq[i], peer_keys1[i], peer_keys2[i], peer_u[i], peer_v[i])
        e = jax.nn.sigmoid(jnp.einsum('bsd,de->bse', x, ple_gate_w[i])) * jnp.einsum('bsc,cd->bsd', p[i], ple_w[i])
        x = layer_norm(ALPHA * x + f + e, ln2_g[i], ln2_b[i])
    return x
```

```python
import functools
import math

import jax
import jax.numpy as jnp
import numpy as np
from jax import lax
from jax.experimental import pallas as pl
from jax.experimental.pallas import tpu as pltpu

F32 = jnp.float32
BF16 = jnp.bfloat16

D_MODEL = 2048
DEPTH = 1
CHUNK = 64
POOL_WINDOWS = (2, 4, 8, 16)
POOL_GROUPS = 4
POOL_WIDTH = D_MODEL // 2
POOL_GROUP_DIM = POOL_WIDTH // POOL_GROUPS
N_HEADS = 8
HEAD_DIM = D_MODEL // 32
V_DIM = 2 * HEAD_DIM
QK_WIDTH = N_HEADS * 2 * HEAD_DIM
ATTN_WIDTH = N_HEADS * V_DIM
N_BRANCHES = 2
GATE_WIDTH = N_BRANCHES * D_MODEL
NUM_BUCKETS = 32
MAX_DISTANCE = 128
PEER_HEADS = 8
N_KEYS = 128
N_EXPERTS = N_KEYS * N_KEYS
PEER_HALF = 128
PEER_TOPK = 16
PLE_DIM = 256
ALPHA = (2 * DEPTH) ** 0.25
LN_EPS = 1e-5
MASK_VALUE = -1e30
LAM_INIT = 0.8 - 0.6 * math.exp(-0.3 * 0)

LANE = 128
SUBLANE = 8
VMEM_LIMIT_BYTES = 60 * 1024 * 1024

TQ = 256
TKV = 256
HALO = 16
TM_IN = 512
TN_IN = 1024
TM_MERGE = 256
TM_PEER = 512
EC_PEER = 512

COL_G = 0
COL_P = GATE_WIDTH
COL_Q = COL_P + POOL_WIDTH
COL_K = COL_Q + QK_WIDTH
NAT_COLS = COL_K + QK_WIDTH


def _layer_norm(x, g, b):
    mu = jnp.mean(x, axis=-1, keepdims=True)
    xc = x - mu
    var = jnp.mean(xc * xc, axis=-1, keepdims=True)
    return xc * lax.rsqrt(var + LN_EPS) * g + b


def _sigmoid(x):
    return 1.0 / (1.0 + jnp.exp(-x))


def _t5_bucket(rel):
    nb = NUM_BUCKETS // 2
    ret = (rel > 0).astype(jnp.int32) * nb
    n = jnp.abs(rel)
    max_exact = nb // 2
    is_small = n < max_exact
    nf = jnp.maximum(n, 1).astype(F32)
    large = max_exact + (jnp.log(nf / max_exact) / math.log(MAX_DISTANCE / max_exact)
                         * (nb - max_exact)).astype(jnp.int32)
    large = jnp.minimum(large, nb - 1)
    return ret + jnp.where(is_small, n, large)


def _bias_kernel(tab_ref, lq1_ref, lk1_ref, lq2_ref, lk2_ref, bias_ref, lam_ref):
    h = pl.program_id(0)
    c = lax.broadcasted_iota(jnp.int32, (2 * TKV, 2 * TQ), 0)
    r = lax.broadcasted_iota(jnp.int32, (2 * TKV, 2 * TQ), 1)
    r = jnp.where(r >= TQ, r - TQ, r)
    rel = c - TKV - r
    bucket = _t5_bucket(rel)
    far = tab_ref[h, NUM_BUCKETS // 2 - 1]
    val = jnp.zeros((2 * TKV, 2 * TQ), F32)
    for b in range(NUM_BUCKETS):
        val = jnp.where(bucket == b, tab_ref[h, b] - far, val)
    shift = CHUNK.bit_length() - 1
    allowed = lax.shift_right_logical(c, shift) <= lax.shift_right_logical(r, shift) + TKV // CHUNK
    bias_ref[...] = jnp.where(allowed, val, MASK_VALUE)
    lam = (jnp.exp(jnp.sum(lq1_ref[...] * lk1_ref[...], axis=-1, keepdims=True))
           - jnp.exp(jnp.sum(lq2_ref[...] * lk2_ref[...], axis=-1, keepdims=True)) + LAM_INIT)
    lam_ref[...] = jnp.broadcast_to(lam, lam_ref.shape)


def _attn_bias(rel_bias, lq1, lk1, lq2, lk2):
    assert TQ == TKV and TQ > MAX_DISTANCE and TQ % CHUNK == 0
    tab = rel_bias.T.astype(F32)
    vec = pl.BlockSpec((1, HEAD_DIM), lambda h: (0, 0))
    return pl.pallas_call(
        _bias_kernel,
        grid=(N_HEADS,),
        in_specs=[pl.BlockSpec(memory_space=pltpu.SMEM), vec, vec, vec, vec],
        out_specs=[pl.BlockSpec((None, 2 * TKV, 2 * TQ), lambda h: (h, 0, 0)),
                   pl.BlockSpec((None, SUBLANE, LANE), lambda h: (h, 0, 0))],
        out_shape=[jax.ShapeDtypeStruct((N_HEADS, 2 * TKV, 2 * TQ), F32),
                   jax.ShapeDtypeStruct((N_HEADS, SUBLANE, LANE), F32)],
        name="attn_bias",
    )(tab, lq1.reshape(1, -1), lk1.reshape(1, -1), lq2.reshape(1, -1), lk2.reshape(1, -1))


def _inproj_kernel(x_ref, g_ref, b_ref, w_ref, wvt_ref, hn_ref, x0_ref, vt_ref, xb_scr):
    j = pl.program_id(1)

    @pl.when(j == 0)
    def _():
        y = _layer_norm(x_ref[...], g_ref[...], b_ref[...])
        x0_ref[...] = y
        yb = y.astype(BF16)
        xb_scr[...] = yb
        vt = lax.dot_general(wvt_ref[...], yb, (((1,), (1,)), ((), ())),
                             preferred_element_type=F32)
        for s in range(vt_ref.shape[0]):
            vt_ref[s] = vt[:, s * TKV:(s + 1) * TKV].astype(BF16)

    hn_ref[...] = jnp.dot(xb_scr[...], w_ref[...], preferred_element_type=F32).astype(BF16)


def _in_proj(x2, ln_g, ln_b, w_nat, w_vt, batch, seq):
    t = x2.shape[0]
    tm, tn = TM_IN, TN_IN
    assert seq % tm == 0 and tm % TKV == 0 and NAT_COLS % tn == 0
    nblk = seq // tm
    const = lambda i, j: (0, 0)
    return pl.pallas_call(
        _inproj_kernel,
        grid=(t // tm, NAT_COLS // tn),
        in_specs=[pl.BlockSpec((tm, D_MODEL), lambda i, j: (i, 0)),
                  pl.BlockSpec((1, D_MODEL), const),
                  pl.BlockSpec((1, D_MODEL), const),
                  pl.BlockSpec((D_MODEL, tn), lambda i, j: (0, j)),
                  pl.BlockSpec((ATTN_WIDTH, D_MODEL), const)],
        out_specs=[pl.BlockSpec((tm, tn), lambda i, j: (i, j)),
                   pl.BlockSpec((tm, D_MODEL), lambda i, j: (i, 0)),
                   pl.BlockSpec((None, tm // TKV, ATTN_WIDTH, TKV),
                                lambda i, j: (i // nblk, i % nblk, 0, 0))],
        out_shape=[jax.ShapeDtypeStruct((t, NAT_COLS), BF16),
                   jax.ShapeDtypeStruct((t, D_MODEL), F32),
                   jax.ShapeDtypeStruct((batch, seq // TKV, ATTN_WIDTH, TKV), BF16)],
        scratch_shapes=[pltpu.VMEM((tm, D_MODEL), BF16)],
        compiler_params=pltpu.CompilerParams(
            dimension_semantics=("arbitrary", "arbitrary"), vmem_limit_bytes=VMEM_LIMIT_BYTES),
        name="in_proj",
    )(x2, ln_g.reshape(1, -1), ln_b.reshape(1, -1), w_nat, w_vt)


def _attn_kernel(lam_ref, q_ref, k_ref, vt_ref, bias_ref, g_ref, o_ref, m_scr, l_scr, acc_scr):
    i = pl.program_id(2)
    qs = q_ref[...].astype(F32) * (HEAD_DIM ** -0.5)
    lane = lax.broadcasted_iota(jnp.int32, qs.shape, 1)
    q2 = jnp.concatenate([jnp.where(lane < HEAD_DIM, qs, 0.0),
                          jnp.where(lane >= HEAD_DIM, qs, 0.0)], axis=0).astype(BF16)

    m_scr[...] = jnp.full(m_scr.shape, -jnp.inf, F32)
    l_scr[...] = jnp.zeros(l_scr.shape, F32)
    acc_scr[...] = jnp.zeros(acc_scr.shape, F32)

    def step(j, bias):
        kblk = k_ref[pl.ds(pl.multiple_of(j * TKV, TKV), TKV), :]
        st = lax.dot_general(kblk, q2, (((1,), (1,)), ((), ())),
                             preferred_element_type=F32)
        if bias is not None:
            st = st + bias
        m_prev = m_scr[...]
        m_new = jnp.maximum(m_prev, jnp.max(st, axis=0, keepdims=True))
        alpha = jnp.exp(m_prev - m_new)
        p = jnp.exp(st - m_new)
        l_scr[...] = alpha * l_scr[...] + jnp.sum(p, axis=0, keepdims=True)
        acc_scr[...] = acc_scr[...] * alpha + jnp.dot(
            vt_ref[j], p.astype(BF16), preferred_element_type=F32)
        m_scr[...] = m_new

    def far_body(j, carry):
        step(j, None)
        return carry

    lax.fori_loop(0, i - 1, far_body, 0)

    @pl.when(i > 0)
    def _():
        step(i - 1, bias_ref[0:TKV, :])

    step(i, bias_ref[TKV:2 * TKV, :])

    o = acc_scr[...] / l_scr[...]
    lam = lam_ref[0:1, 0:1]
    att = o[:, :TQ] - lam * o[:, TQ:]
    ms = jnp.mean(att * att, axis=0, keepdims=True)
    y = att * lax.rsqrt(ms + LN_EPS) * g_ref[...] * (1.0 - LAM_INIT)
    o_ref[...] = y.T.astype(BF16)


def _diff_attn(lam, hn, vt, bias, subln_g, batch, seq):
    t = hn.shape[0]
    nq = seq // TQ
    qb, kb = COL_Q // LANE, COL_K // LANE
    return pl.pallas_call(
        _attn_kernel,
        grid=(batch, N_HEADS, nq),
        in_specs=[pl.BlockSpec((None, SUBLANE, LANE), lambda b, h, i: (h, 0, 0)),
                  pl.BlockSpec((TQ, LANE), lambda b, h, i: (b * nq + i, qb + h)),
                  pl.BlockSpec((seq, LANE), lambda b, h, i: (b, kb + h)),
                  pl.BlockSpec((None, seq // TKV, V_DIM, TKV), lambda b, h, i: (b, 0, h, 0)),
                  pl.BlockSpec((None, 2 * TKV, 2 * TQ), lambda b, h, i: (h, 0, 0)),
                  pl.BlockSpec((V_DIM, 1), lambda b, h, i: (0, 0))],
        out_specs=pl.BlockSpec((TQ, V_DIM), lambda b, h, i: (b * nq + i, h)),
        out_shape=jax.ShapeDtypeStruct((t, ATTN_WIDTH), BF16),
        scratch_shapes=[pltpu.VMEM((1, 2 * TQ), F32), pltpu.VMEM((1, 2 * TQ), F32),
                        pltpu.VMEM((V_DIM, 2 * TQ), F32)],
        compiler_params=pltpu.CompilerParams(
            dimension_semantics=("arbitrary", "arbitrary", "arbitrary"),
            vmem_limit_bytes=VMEM_LIMIT_BYTES),
        name="diff_attn",
    )(lam, hn, hn, vt, bias, subln_g.reshape(-1, 1))


def _merge_kernel(gp_ref, xp_ref, halo_ref, ya_ref, x0_ref, p_ref, wgrp_ref, psc_ref, wbp_ref,
                  wba_ref, gb_ref, wout_ref, l1g_ref, l1b_ref, wg_ref, wp_ref,
                  x1_ref, r_ref, *, nblk):
    i = pl.program_id(0)
    tm = xp_ref.shape[0]
    blk_in_seq = i % nblk
    xp = xp_ref[...].astype(F32)
    halo = halo_ref[...].astype(F32)
    halo = jnp.where(blk_in_seq == 0, jnp.zeros_like(halo), halo)
    ext = jnp.concatenate([halo, xp], axis=0)
    pos = blk_in_seq * tm + lax.broadcasted_iota(jnp.int32, (tm, 1), 0)

    yps = []
    for gi, w in enumerate(POOL_WINDOWS):
        lo, hi = gi * POOL_GROUP_DIM, (gi + 1) * POOL_GROUP_DIM
        s = ext[:, lo:hi]
        span = 1
        while span < w:
            s = s[span:] + s[:-span]
            span *= 2
        start = HALO - (w - 1)
        win = s[start:start + tm]
        cnt = jnp.minimum(pos + 1, w).astype(F32)
        mixed = win / cnt - xp[:, lo:hi]
        yps.append(jnp.dot(mixed.astype(BF16), wgrp_ref[gi], preferred_element_type=F32))
    yp = jnp.concatenate(yps, axis=1) * psc_ref[...]

    gates = _sigmoid(gp_ref[...].astype(F32) + gb_ref[...])
    bp = jnp.dot(yp.astype(BF16), wbp_ref[...], preferred_element_type=F32)
    ba = jnp.dot(ya_ref[...], wba_ref[...], preferred_element_type=F32)
    merged = gates[:, :D_MODEL] * bp + gates[:, D_MODEL:] * ba
    y = jnp.dot(merged.astype(BF16), wout_ref[...], preferred_element_type=F32)
    x1 = _layer_norm(ALPHA * x0_ref[...] + y, l1g_ref[...], l1b_ref[...])
    x1_ref[...] = x1

    x1b = x1.astype(BF16)
    e = (_sigmoid(jnp.dot(x1b, wg_ref[...], preferred_element_type=F32))
         * jnp.dot(p_ref[...].astype(BF16), wp_ref[...], preferred_element_type=F32))
    r_ref[...] = ALPHA * x1 + e


def _merge(hn, ya, x0, p2, wgrp, psc, wbp, wba, gate_b, wout, l1g, l1b, wg, wp, seq):
    t = hn.shape[0]
    tm = TM_MERGE
    assert seq % tm == 0 and tm % HALO == 0
    nblk = seq // tm
    const2 = lambda i: (0, 0)
    single = pl.Buffered(1)
    wspec = lambda shape: pl.BlockSpec(shape, (lambda i: (0,) * len(shape)), pipeline_mode=single)
    return pl.pallas_call(
        functools.partial(_merge_kernel, nblk=nblk),
        grid=(t // tm,),
        in_specs=[pl.BlockSpec((tm, GATE_WIDTH), lambda i: (i, COL_G // GATE_WIDTH)),
                  pl.BlockSpec((tm, POOL_WIDTH), lambda i: (i, COL_P // POOL_WIDTH)),
                  pl.BlockSpec((HALO, POOL_WIDTH),
                               lambda i: (jnp.maximum(i * (tm // HALO) - 1, 0), COL_P // POOL_WIDTH)),
                  pl.BlockSpec((tm, ATTN_WIDTH), lambda i: (i, 0)),
                  pl.BlockSpec((tm, D_MODEL), lambda i: (i, 0)),
                  pl.BlockSpec((tm, PLE_DIM), lambda i: (i, 0)),
                  wspec((POOL_GROUPS, POOL_GROUP_DIM, POOL_GROUP_DIM)),
                  wspec((1, POOL_WIDTH)),
                  wspec((POOL_WIDTH, D_MODEL)),
                  wspec((ATTN_WIDTH, D_MODEL)),
                  wspec((1, GATE_WIDTH)),
                  wspec((D_MODEL, D_MODEL)),
                  wspec((1, D_MODEL)),
                  wspec((1, D_MODEL)),
                  wspec((D_MODEL, D_MODEL)),
                  wspec((PLE_DIM, D_MODEL))],
        out_specs=[pl.BlockSpec((tm, D_MODEL), lambda i: (i, 0)),
                   pl.BlockSpec((tm, D_MODEL), lambda i: (i, 0))],
        out_shape=[jax.ShapeDtypeStruct((t, D_MODEL), F32),
                   jax.ShapeDtypeStruct((t, D_MODEL), F32)],
        compiler_params=pltpu.CompilerParams(
            dimension_semantics=("arbitrary",), vmem_limit_bytes=VMEM_LIMIT_BYTES),
        name="merge",
    )(hn, hn, hn, ya, x0, p2, wgrp, psc, wbp, wba, gate_b, wout, l1g, l1b, wg, wp)


_NTOP = PEER_TOPK + 1
_CAND = [(a, b) for a in range(_NTOP) for b in range(_NTOP)
         if (a + 1) * (b + 1) <= _NTOP]
_CAND_ROWS = -(-len(_CAND) // SUBLANE) * SUBLANE


def _top_values(s, k):
    vals = []
    for _ in range(k):
        m = jnp.max(s, axis=0, keepdims=True)
        vals.append(m)
        s = jnp.where(s == m, -jnp.inf, s)
    return vals


def _gelu(x):
    return 0.5 * x * (1.0 + lax.erf(x * np.float32(math.sqrt(0.5))))


def _peer_kernel(x1_ref, r_ref, wqt_ref, k1_ref, k2_ref, u_ref, vt_ref, g_ref, b_ref, o_ref,
                 xt_scr, thr_scr, e1_scr, s2_scr, e2_scr, cand_scr, acc_scr):
    c = pl.program_id(1)
    tm = x1_ref.shape[0]
    ec = u_ref.shape[0]

    @pl.when(c == 0)
    def _():
        xt = x1_ref[...].T.astype(BF16)
        xt_scr[...] = xt
        cand_scr[...] = jnp.full(cand_scr.shape, -jnp.inf, F32)
        qt = jnp.dot(wqt_ref[...], xt, preferred_element_type=F32)
        for h in range(PEER_HEADS):
            base = h * 2 * PEER_HALF
            s1 = jnp.dot(k1_ref[...], qt[base:base + PEER_HALF].astype(BF16),
                         preferred_element_type=F32)
            s2 = jnp.dot(k2_ref[...], qt[base + PEER_HALF:base + 2 * PEER_HALF].astype(BF16),
                         preferred_element_type=F32)
            v1 = _top_values(s1, _NTOP)
            v2 = _top_values(s2, _NTOP)
            for n, (a, b) in enumerate(_CAND):
                cand_scr[n:n + 1, :] = v1[a] + v2[b]
            cand = cand_scr[...]
            top = _top_values(cand, _NTOP)
            tau = 0.5 * (top[PEER_TOPK - 1] + top[PEER_TOPK])
            cmax = top[0]
            z = jnp.sum(jnp.where(cand >= tau, jnp.exp(cand - cmax), 0.0), axis=0, keepdims=True)
            thr_scr[h] = tau - s1
            e1_scr[h] = jnp.exp(s1 - v1[0])
            s2_scr[h] = s2
            e2_scr[h] = jnp.exp(s2 - v2[0]) / z
        acc_scr[...] = jnp.zeros(acc_scr.shape, F32)

    ht = jnp.dot(u_ref[...], xt_scr[...], preferred_element_type=F32)
    act = _gelu(ht)
    parts = []
    for al in range(ec // N_KEYS):
        ka = c * (ec // N_KEYS) + al
        w = jnp.zeros((N_KEYS, tm), F32)
        for h in range(PEER_HEADS):
            thr = thr_scr[h, pl.ds(ka, 1), :]
            e1 = e1_scr[h, pl.ds(ka, 1), :]
            w = w + jnp.where(s2_scr[h] >= thr, e2_scr[h], 0.0) * e1
        parts.append((act[al * N_KEYS:(al + 1) * N_KEYS] * w).astype(BF16))
    at = jnp.concatenate(parts, axis=0)
    acc_scr[...] += jnp.dot(vt_ref[...], at, preferred_element_type=F32)

    @pl.when(c == pl.num_programs(1) - 1)
    def _():
        y = r_ref[...] + acc_scr[...].T
        o_ref[...] = _layer_norm(y, g_ref[...], b_ref[...])


def _peer(x1, r, wqt, k1, k2, u, vt, l2g, l2b):
    t = x1.shape[0]
    tm, ec = TM_PEER, EC_PEER
    assert t % tm == 0 and N_EXPERTS % ec == 0 and ec % N_KEYS == 0
    single = pl.Buffered(1)
    row = lambda i, c: (i, 0)
    const = lambda i, c: (0, 0)
    stat = pltpu.VMEM((PEER_HEADS, N_KEYS, tm), F32)
    return pl.pallas_call(
        _peer_kernel,
        grid=(t // tm, N_EXPERTS // ec),
        in_specs=[pl.BlockSpec((tm, D_MODEL), row, pipeline_mode=single),
                  pl.BlockSpec((tm, D_MODEL), row, pipeline_mode=single),
                  pl.BlockSpec((PEER_HEADS * 2 * PEER_HALF, D_MODEL), const, pipeline_mode=single),
                  pl.BlockSpec((N_KEYS, PEER_HALF), const, pipeline_mode=single),
                  pl.BlockSpec((N_KEYS, PEER_HALF), const, pipeline_mode=single),
                  pl.BlockSpec((ec, D_MODEL), lambda i, c: (c, 0)),
                  pl.BlockSpec((D_MODEL, ec), lambda i, c: (0, c)),
                  pl.BlockSpec((1, D_MODEL), const, pipeline_mode=single),
                  pl.BlockSpec((1, D_MODEL), const, pipeline_mode=single)],
        out_specs=pl.BlockSpec((tm, D_MODEL), row),
        out_shape=jax.ShapeDtypeStruct((t, D_MODEL), F32),
        scratch_shapes=[pltpu.VMEM((D_MODEL, tm), BF16), stat, stat, stat, stat,
                        pltpu.VMEM((_CAND_ROWS, tm), F32), pltpu.VMEM((D_MODEL, tm), F32)],
        compiler_params=pltpu.CompilerParams(
            dimension_semantics=("arbitrary", "arbitrary"), vmem_limit_bytes=VMEM_LIMIT_BYTES),
        name="peer",
    )(x1, r, wqt, k1, k2, u, vt, l2g.reshape(1, -1), l2b.reshape(1, -1))


def kernel(x, p, ln_in_g, ln_in_b, w_in, pool_grp_w, pool_scale, lambda_q1, lambda_k1,
           lambda_q2, lambda_k2, subln_g, w_branch_pool, w_branch_attn, gate_b, w_out,
           ln1_g, ln1_b, peer_wq, peer_keys1, peer_keys2, peer_u, peer_v, ple_w, ple_gate_w,
           ln2_g, ln2_b, rel_bias):
    batch, seq, d = x.shape
    assert d == D_MODEL and w_in.shape[0] == DEPTH == 1
    t = batch * seq
    c0, c1 = POOL_WIDTH, POOL_WIDTH + QK_WIDTH
    c2 = c1 + QK_WIDTH
    c3 = c2 + ATTN_WIDTH

    w = w_in[0]
    w_nat = jnp.concatenate([w[:, c3:], w[:, :c0], w[:, c0:c1], w[:, c1:c2]], axis=1).astype(BF16)
    w_vt = w[:, c2:c3].T.astype(BF16)

    bias, lam = _attn_bias(rel_bias, lambda_q1[0], lambda_k1[0], lambda_q2[0], lambda_k2[0])
    hn, x0, vt = _in_proj(x.reshape(t, d), ln_in_g, ln_in_b, w_nat, w_vt, batch, seq)
    ya = _diff_attn(lam, hn, vt, bias, subln_g[0], batch, seq)
    x1, r = _merge(hn, ya, x0, p[0].reshape(t, PLE_DIM),
                   pool_grp_w[0].astype(BF16), pool_scale[0].reshape(1, -1),
                   w_branch_pool[0].astype(BF16), w_branch_attn[0].astype(BF16),
                   gate_b[0].reshape(1, -1), w_out[0].astype(BF16),
                   ln1_g[0].reshape(1, -1), ln1_b[0].reshape(1, -1),
                   ple_gate_w[0].astype(BF16), ple_w[0].astype(BF16), seq)
    out = _peer(x1, r, peer_wq[0].T.astype(BF16), peer_keys1[0].astype(BF16),
                peer_keys2[0].astype(BF16), peer_u[0].astype(BF16), peer_v[0].T.astype(BF16),
                ln2_g[0], ln2_b[0])
    return out.reshape(batch, seq, d)
```

```python
import functools
import math

import jax
import jax.numpy as jnp
import numpy as np
from jax import lax
from jax.experimental import pallas as pl
from jax.experimental.pallas import tpu as pltpu

F32 = jnp.float32
BF16 = jnp.bfloat16

D_MODEL = 2048
DEPTH = 1
CHUNK = 64
POOL_WINDOWS = (2, 4, 8, 16)
POOL_GROUPS = 4
POOL_WIDTH = D_MODEL // 2
POOL_GROUP_DIM = POOL_WIDTH // POOL_GROUPS
N_HEADS = 8
HEAD_DIM = D_MODEL // 32
V_DIM = 2 * HEAD_DIM
QK_WIDTH = N_HEADS * 2 * HEAD_DIM
ATTN_WIDTH = N_HEADS * V_DIM
N_BRANCHES = 2
GATE_WIDTH = N_BRANCHES * D_MODEL
NUM_BUCKETS = 32
MAX_DISTANCE = 128
PEER_HEADS = 8
N_KEYS = 128
N_EXPERTS = N_KEYS * N_KEYS
PEER_HALF = 128
PEER_TOPK = 16
PLE_DIM = 256
ALPHA = (2 * DEPTH) ** 0.25
LN_EPS = 1e-5
MASK_VALUE = -1e30
LAM_INIT = 0.8 - 0.6 * math.exp(-0.3 * 0)

LANE = 128
SUBLANE = 8
VMEM_LIMIT_BYTES = 60 * 1024 * 1024

V_AUG = V_DIM + 16
LOG2E = math.log2(math.e)
TQ = 256
TKV = 256
HALO = 16
TM_IN = 512
TN_IN = 1024
TM_MERGE = 256
TM_PEER = 512
EC_PEER = 512

COL_G = 0
COL_P = GATE_WIDTH
COL_Q = COL_P + POOL_WIDTH
COL_K = COL_Q + QK_WIDTH
NAT_COLS = COL_K + QK_WIDTH


def _layer_norm(x, g, b):
    mu = jnp.mean(x, axis=-1, keepdims=True)
    xc = x - mu
    var = jnp.mean(xc * xc, axis=-1, keepdims=True)
    return xc * lax.rsqrt(var + LN_EPS) * g + b


def _sigmoid(x):
    return 1.0 / (1.0 + jnp.exp(-x))


def _t5_bucket(rel):
    nb = NUM_BUCKETS // 2
    ret = (rel > 0).astype(jnp.int32) * nb
    n = jnp.abs(rel)
    max_exact = nb // 2
    is_small = n < max_exact
    nf = jnp.maximum(n, 1).astype(F32)
    large = max_exact + (jnp.log(nf / max_exact) / math.log(MAX_DISTANCE / max_exact)
                         * (nb - max_exact)).astype(jnp.int32)
    large = jnp.minimum(large, nb - 1)
    return ret + jnp.where(is_small, n, large)


def _bias_kernel(tab_ref, lq1_ref, lk1_ref, lq2_ref, lk2_ref, bias_ref, lam_ref):
    h = pl.program_id(0)
    shape = (2 * TKV, 2 * TQ)
    c = lax.broadcasted_iota(jnp.int32, shape, 0)
    r = lax.broadcasted_iota(jnp.int32, shape, 1)
    r = jnp.where(r >= TQ, r - TQ, r)
    rel = c - TKV - r
    bucket = _t5_bucket(rel)
    far = tab_ref[h, NUM_BUCKETS // 2 - 1]
    val = jnp.zeros(shape, F32)
    for b in range(NUM_BUCKETS):
        val = jnp.where(bucket == b, (tab_ref[h, b] - far) * LOG2E, val)
    shift = CHUNK.bit_length() - 1
    allowed = lax.shift_right_logical(c, shift) <= lax.shift_right_logical(r, shift) + TKV // CHUNK
    bias_ref[0:TKV, :] = jnp.zeros((TKV, 2 * TQ), F32)
    bias_ref[TKV:3 * TKV, :] = jnp.where(allowed, val, MASK_VALUE)
    lam = (jnp.exp(jnp.sum(lq1_ref[...] * lk1_ref[...], axis=-1, keepdims=True))
           - jnp.exp(jnp.sum(lq2_ref[...] * lk2_ref[...], axis=-1, keepdims=True)) + LAM_INIT)
    lam_ref[...] = jnp.broadcast_to(lam, lam_ref.shape)


def _attn_bias(rel_bias, lq1, lk1, lq2, lk2):
    assert TQ == TKV and TQ > MAX_DISTANCE and TQ % CHUNK == 0
    tab = rel_bias.T.astype(F32)
    vec = pl.BlockSpec((1, HEAD_DIM), lambda h: (0, 0))
    return pl.pallas_call(
        _bias_kernel,
        grid=(N_HEADS,),
        in_specs=[pl.BlockSpec(memory_space=pltpu.SMEM), vec, vec, vec, vec],
        out_specs=[pl.BlockSpec((None, 3 * TKV, 2 * TQ), lambda h: (h, 0, 0)),
                   pl.BlockSpec((None, SUBLANE, LANE), lambda h: (h, 0, 0))],
        out_shape=[jax.ShapeDtypeStruct((N_HEADS, 3 * TKV, 2 * TQ), F32),
                   jax.ShapeDtypeStruct((N_HEADS, SUBLANE, LANE), F32)],
        name="attn_bias",
    )(tab, lq1.reshape(1, -1), lk1.reshape(1, -1), lq2.reshape(1, -1), lk2.reshape(1, -1))


def _inproj_kernel(x_ref, g_ref, b_ref, w_ref, wvt_ref, hn_ref, x0_ref, vt_ref, xb_scr):
    j = pl.program_id(1)

    @pl.when(j == 0)
    def _():
        y = _layer_norm(x_ref[...], g_ref[...], b_ref[...])
        x0_ref[...] = y
        yb = y.astype(BF16)
        xb_scr[...] = yb
        vt = lax.dot_general(wvt_ref[...], yb, (((1,), (1,)), ((), ())),
                             preferred_element_type=F32)
        row = lax.broadcasted_iota(jnp.int32, (V_AUG - V_DIM, TKV), 0)
        ones_row = jnp.where(row == 0, 1.0, 0.0).astype(BF16)
        for s in range(vt_ref.shape[0]):
            for h in range(N_HEADS):
                vt_ref[s, h, 0:V_DIM, :] = vt[h * V_DIM:(h + 1) * V_DIM,
                                              s * TKV:(s + 1) * TKV].astype(BF16)
                vt_ref[s, h, V_DIM:V_AUG, :] = ones_row

    hn_ref[...] = jnp.dot(xb_scr[...], w_ref[...], preferred_element_type=F32).astype(BF16)


def _in_proj(x2, ln_g, ln_b, w_nat, w_vt, batch, seq):
    t = x2.shape[0]
    tm, tn = TM_IN, TN_IN
    assert seq % tm == 0 and tm % TKV == 0 and NAT_COLS % tn == 0
    nblk = seq // tm
    const = lambda i, j: (0, 0)
    return pl.pallas_call(
        _inproj_kernel,
        grid=(t // tm, NAT_COLS // tn),
        in_specs=[pl.BlockSpec((tm, D_MODEL), lambda i, j: (i, 0)),
                  pl.BlockSpec((1, D_MODEL), const),
                  pl.BlockSpec((1, D_MODEL), const),
                  pl.BlockSpec((D_MODEL, tn), lambda i, j: (0, j)),
                  pl.BlockSpec((ATTN_WIDTH, D_MODEL), const)],
        out_specs=[pl.BlockSpec((tm, tn), lambda i, j: (i, j)),
                   pl.BlockSpec((tm, D_MODEL), lambda i, j: (i, 0)),
                   pl.BlockSpec((None, tm // TKV, N_HEADS, V_AUG, TKV),
                                lambda i, j: (i // nblk, i % nblk, 0, 0, 0))],
        out_shape=[jax.ShapeDtypeStruct((t, NAT_COLS), BF16),
                   jax.ShapeDtypeStruct((t, D_MODEL), F32),
                   jax.ShapeDtypeStruct((batch, seq // TKV, N_HEADS, V_AUG, TKV), BF16)],
        scratch_shapes=[pltpu.VMEM((tm, D_MODEL), BF16)],
        compiler_params=pltpu.CompilerParams(
            dimension_semantics=("arbitrary", "arbitrary"), vmem_limit_bytes=VMEM_LIMIT_BYTES),
        name="in_proj",
    )(x2, ln_g.reshape(1, -1), ln_b.reshape(1, -1), w_nat, w_vt)


def _attn_kernel(lam_ref, q_ref, k_ref, vt_ref, bias_ref, g_ref, o_ref,
                 q2_scr, s_scr, p_scr, al_scr, m_scr, acc_scr):
    i = pl.program_id(2)
    n = i + 1
    qs = q_ref[...].astype(F32)
    lane = lax.broadcasted_iota(jnp.int32, qs.shape, 1)
    q2_scr[...] = jnp.concatenate([jnp.where(lane < HEAD_DIM, qs, 0.0),
                                   jnp.where(lane >= HEAD_DIM, qs, 0.0)], axis=0).astype(BF16)

    m_scr[...] = jnp.full(m_scr.shape, -jnp.inf, F32)
    acc_scr[...] = jnp.zeros(acc_scr.shape, F32)
    s_scr[1] = jnp.zeros(s_scr.shape[1:], F32)
    p_scr[0] = jnp.zeros(p_scr.shape[1:], BF16)
    al_scr[0] = jnp.ones(al_scr.shape[1:], F32)

    def substep(t, sa):
        sb = 1 - sa
        pv = jnp.dot(vt_ref[jnp.clip(t - 2, 0, n - 1)], p_scr[sa],
                     preferred_element_type=F32)
        acc_scr[...] = acc_scr[...] * al_scr[sa] + pv
        live_b = jnp.logical_and(t >= 1, t <= n)
        st = s_scr[sb]
        m_prev = m_scr[...]
        m_new = jnp.where(live_b, jnp.maximum(m_prev, jnp.max(st, axis=0, keepdims=True)), m_prev)
        al_scr[sb] = jnp.where(live_b, jnp.exp2(m_prev - m_new), 1.0)
        m_scr[...] = m_new
        p_scr[sb] = jnp.exp2(st - jnp.where(live_b, m_new, jnp.inf)).astype(BF16)
        ja = jnp.minimum(t, n - 1)
        kblk = k_ref[pl.ds(pl.multiple_of(ja * TKV, TKV), TKV), :]
        boff = jnp.clip(ja - (i - 2), 0, 2) * TKV
        s_scr[sa] = lax.dot_general(kblk, q2_scr[...], (((1,), (1,)), ((), ())),
                                    preferred_element_type=F32) + bias_ref[pl.ds(pl.multiple_of(boff, TKV), TKV), :]

    def body(tt, carry):
        substep(2 * tt, 0)
        substep(2 * tt + 1, 1)
        return carry

    lax.fori_loop(0, (n + 3) // 2, body, 0)

    o = acc_scr[0:V_DIM, :] / acc_scr[V_DIM:V_DIM + 1, :]
    lam = lam_ref[0:1, 0:1]
    att = o[:, :TQ] - lam * o[:, TQ:]
    ms = jnp.mean(att * att, axis=0, keepdims=True)
    y = att * lax.rsqrt(ms + LN_EPS) * g_ref[...] * (1.0 - LAM_INIT)
    o_ref[...] = y.T.astype(BF16)


def _diff_attn(lam, hn, vt, bias, subln_g, batch, seq):
    t = hn.shape[0]
    nq = seq // TQ
    qb, kb = COL_Q // LANE, COL_K // LANE
    return pl.pallas_call(
        _attn_kernel,
        grid=(batch, N_HEADS, nq),
        in_specs=[pl.BlockSpec((None, SUBLANE, LANE), lambda b, h, i: (h, 0, 0)),
                  pl.BlockSpec((TQ, LANE), lambda b, h, i: (b * nq + i, qb + h)),
                  pl.BlockSpec((seq, LANE), lambda b, h, i: (b, kb + h)),
                  pl.BlockSpec((None, seq // TKV, None, V_AUG, TKV), lambda b, h, i: (b, 0, h, 0, 0)),
                  pl.BlockSpec((None, 3 * TKV, 2 * TQ), lambda b, h, i: (h, 0, 0)),
                  pl.BlockSpec((V_DIM, 1), lambda b, h, i: (0, 0))],
        out_specs=pl.BlockSpec((TQ, V_DIM), lambda b, h, i: (b * nq + i, h)),
        out_shape=jax.ShapeDtypeStruct((t, ATTN_WIDTH), BF16),
        scratch_shapes=[pltpu.VMEM((2 * TQ, LANE), BF16),
                        pltpu.VMEM((2, TKV, 2 * TQ), F32), pltpu.VMEM((2, TKV, 2 * TQ), BF16),
                        pltpu.VMEM((2, 1, 2 * TQ), F32),
                        pltpu.VMEM((1, 2 * TQ), F32),
                        pltpu.VMEM((V_AUG, 2 * TQ), F32)],
        compiler_params=pltpu.CompilerParams(
            dimension_semantics=("arbitrary", "arbitrary", "arbitrary"),
            vmem_limit_bytes=VMEM_LIMIT_BYTES),
        name="diff_attn",
    )(lam, hn, hn, vt, bias, subln_g.reshape(-1, 1))


def _merge_kernel(gp_ref, xp_ref, halo_ref, ya_ref, x0_ref, p_ref, wgrp_ref, psc_ref, wbp_ref,
                  wba_ref, gb_ref, wout_ref, l1g_ref, l1b_ref, wg_ref, wp_ref,
                  x1_ref, r_ref, *, nblk):
    i = pl.program_id(0)
    tm = xp_ref.shape[0]
    blk_in_seq = i % nblk
    xp = xp_ref[...].astype(F32)
    halo = halo_ref[...].astype(F32)
    halo = jnp.where(blk_in_seq == 0, jnp.zeros_like(halo), halo)
    ext = jnp.concatenate([halo, xp], axis=0)
    pos = blk_in_seq * tm + lax.broadcasted_iota(jnp.int32, (tm, 1), 0)

    yps = []
    for gi, w in enumerate(POOL_WINDOWS):
        lo, hi = gi * POOL_GROUP_DIM, (gi + 1) * POOL_GROUP_DIM
        s = ext[:, lo:hi]
        span = 1
        while span < w:
            s = s[span:] + s[:-span]
            span *= 2
        start = HALO - (w - 1)
        win = s[start:start + tm]
        cnt = jnp.minimum(pos + 1, w).astype(F32)
        mixed = win / cnt - xp[:, lo:hi]
        yps.append(jnp.dot(mixed.astype(BF16), wgrp_ref[gi], preferred_element_type=F32))
    yp = jnp.concatenate(yps, axis=1) * psc_ref[...]

    gates = _sigmoid(gp_ref[...].astype(F32) + gb_ref[...])
    bp = jnp.dot(yp.astype(BF16), wbp_ref[...], preferred_element_type=F32)
    ba = jnp.dot(ya_ref[...], wba_ref[...], preferred_element_type=F32)
    merged = gates[:, :D_MODEL] * bp + gates[:, D_MODEL:] * ba
    y = jnp.dot(merged.astype(BF16), wout_ref[...], preferred_element_type=F32)
    x1 = _layer_norm(ALPHA * x0_ref[...] + y, l1g_ref[...], l1b_ref[...])
    x1_ref[...] = x1

    x1b = x1.astype(BF16)
    e = (_sigmoid(jnp.dot(x1b, wg_ref[...], preferred_element_type=F32))
         * jnp.dot(p_ref[...].astype(BF16), wp_ref[...], preferred_element_type=F32))
    r_ref[...] = ALPHA * x1 + e


def _merge(hn, ya, x0, p2, wgrp, psc, wbp, wba, gate_b, wout, l1g, l1b, wg, wp, seq):
    t = hn.shape[0]
    tm = TM_MERGE
    assert seq % tm == 0 and tm % HALO == 0
    nblk = seq // tm
    const2 = lambda i: (0, 0)
    single = pl.Buffered(1)
    wspec = lambda shape: pl.BlockSpec(shape, (lambda i: (0,) * len(shape)), pipeline_mode=single)
    return pl.pallas_call(
        functools.partial(_merge_kernel, nblk=nblk),
        grid=(t // tm,),
        in_specs=[pl.BlockSpec((tm, GATE_WIDTH), lambda i: (i, COL_G // GATE_WIDTH)),
                  pl.BlockSpec((tm, POOL_WIDTH), lambda i: (i, COL_P // POOL_WIDTH)),
                  pl.BlockSpec((HALO, POOL_WIDTH),
                               lambda i: (jnp.maximum(i * (tm // HALO) - 1, 0), COL_P // POOL_WIDTH)),
                  pl.BlockSpec((tm, ATTN_WIDTH), lambda i: (i, 0)),
                  pl.BlockSpec((tm, D_MODEL), lambda i: (i, 0)),
                  pl.BlockSpec((tm, PLE_DIM), lambda i: (i, 0)),
                  wspec((POOL_GROUPS, POOL_GROUP_DIM, POOL_GROUP_DIM)),
                  wspec((1, POOL_WIDTH)),
                  wspec((POOL_WIDTH, D_MODEL)),
                  wspec((ATTN_WIDTH, D_MODEL)),
                  wspec((1, GATE_WIDTH)),
                  wspec((D_MODEL, D_MODEL)),
                  wspec((1, D_MODEL)),
                  wspec((1, D_MODEL)),
                  wspec((D_MODEL, D_MODEL)),
                  wspec((PLE_DIM, D_MODEL))],
        out_specs=[pl.BlockSpec((tm, D_MODEL), lambda i: (i, 0)),
                   pl.BlockSpec((tm, D_MODEL), lambda i: (i, 0))],
        out_shape=[jax.ShapeDtypeStruct((t, D_MODEL), F32),
                   jax.ShapeDtypeStruct((t, D_MODEL), F32)],
        compiler_params=pltpu.CompilerParams(
            dimension_semantics=("arbitrary",), vmem_limit_bytes=VMEM_LIMIT_BYTES),
        name="merge",
    )(hn, hn, hn, ya, x0, p2, wgrp, psc, wbp, wba, gate_b, wout, l1g, l1b, wg, wp)


_NTOP = PEER_TOPK + 1
_CAND = [(a, b) for a in range(_NTOP) for b in range(_NTOP)
         if (a + 1) * (b + 1) <= _NTOP]
_CAND_ROWS = -(-len(_CAND) // SUBLANE) * SUBLANE


def _top_values(s, k):
    vals = []
    for _ in range(k):
        m = jnp.max(s, axis=0, keepdims=True)
        vals.append(m)
        s = jnp.where(s == m, -jnp.inf, s)
    return vals


def _gelu(x):
    return 0.5 * x * (1.0 + lax.erf(x * np.float32(math.sqrt(0.5))))


def _peer_kernel(x1_ref, r_ref, wqt_ref, k1_ref, k2_ref, u_ref, vt_ref, g_ref, b_ref, o_ref,
                 xt_scr, thr_scr, s2_scr, kap_scr, cand_scr, h_scr, a_scr, acc_scr, *, nc):
    c = pl.program_id(1)
    tm = x1_ref.shape[0]
    ec = u_ref.shape[0]

    @pl.when(c == 0)
    def _():
        xt = x1_ref[...].T.astype(BF16)
        xt_scr[...] = xt
        cand_scr[...] = jnp.full(cand_scr.shape, -jnp.inf, F32)
        qt = jnp.dot(wqt_ref[...], xt, preferred_element_type=F32)
        for h in range(PEER_HEADS):
            base = h * 2 * PEER_HALF
            s1 = jnp.dot(k1_ref[...], qt[base:base + PEER_HALF].astype(BF16),
                         preferred_element_type=F32)
            s2 = jnp.dot(k2_ref[...], qt[base + PEER_HALF:base + 2 * PEER_HALF].astype(BF16),
                         preferred_element_type=F32)
            v1 = _top_values(s1, _NTOP)
            v2 = _top_values(s2, _NTOP)
            for n, (a, b) in enumerate(_CAND):
                cand_scr[n:n + 1, :] = v1[a] + v2[b]
            cand = cand_scr[...]
            top = _top_values(cand, _NTOP)
            tau = 0.5 * (top[PEER_TOPK - 1] + top[PEER_TOPK])
            cmax = top[0]
            z = jnp.sum(jnp.where(cand >= tau, jnp.exp(cand - cmax), 0.0), axis=0, keepdims=True)
            logz = cmax + jnp.log(z)
            thr = (logz - s1) * LOG2E
            for lt in range(tm // LANE):
                thr_scr[h, lt] = thr[:, lt * LANE:(lt + 1) * LANE]
            s2_scr[h] = s2 * LOG2E
            kap_scr[h] = (tau - logz) * LOG2E
        acc_scr[...] = jnp.zeros(acc_scr.shape, F32)
        h_scr[1] = jnp.zeros(h_scr.shape[1:], F32)
        a_scr[0] = jnp.zeros(a_scr.shape[1:], BF16)

    slot = c % 2
    acc_scr[...] += jnp.dot(vt_ref[...], a_scr[slot], preferred_element_type=F32)
    cb = jnp.clip(c - 1, 0, nc - 1)
    live = jnp.logical_and(c >= 1, c <= nc)
    nal = ec // N_KEYS
    kb_rows = N_KEYS // 2
    for lt in range(tm // LANE):
        cols = slice(lt * LANE, (lt + 1) * LANE)
        for half in range(N_KEYS // kb_rows):
            rows = slice(half * kb_rows, (half + 1) * kb_rows)
            w = [jnp.zeros((kb_rows, LANE), F32) for _ in range(nal)]
            for h in range(PEER_HEADS):
                s2t = s2_scr[h, rows, cols]
                kap = kap_scr[h, :, cols]
                for al in range(nal):
                    thr = thr_scr[h, lt, pl.ds(cb * nal + al, 1), :]
                    x = s2t - thr
                    w[al] = w[al] + jnp.where(x >= kap, jnp.exp2(x), 0.0)
            for al in range(nal):
                r0 = al * N_KEYS + half * kb_rows
                a = _gelu(h_scr[1 - slot, r0:r0 + kb_rows, cols]) * w[al]
                a_scr[1 - slot, r0:r0 + kb_rows, cols] = jnp.where(live, a, 0.0).astype(BF16)
    h_scr[slot] = jnp.dot(u_ref[...], xt_scr[...], preferred_element_type=F32)

    @pl.when(c == pl.num_programs(1) - 1)
    def _():
        y = r_ref[...] + acc_scr[...].T
        o_ref[...] = _layer_norm(y, g_ref[...], b_ref[...])


def _peer(x1, r, wqt, k1, k2, u, vt, l2g, l2b):
    t = x1.shape[0]
    tm, ec = TM_PEER, EC_PEER
    assert t % tm == 0 and N_EXPERTS % ec == 0 and ec % N_KEYS == 0
    nc = N_EXPERTS // ec
    single = pl.Buffered(1)
    row = lambda i, c: (i, 0)
    const = lambda i, c: (0, 0)
    stat = pltpu.VMEM((PEER_HEADS, N_KEYS, tm), F32)
    return pl.pallas_call(
        functools.partial(_peer_kernel, nc=nc),
        grid=(t // tm, nc + 2),
        in_specs=[pl.BlockSpec((tm, D_MODEL), row, pipeline_mode=single),
                  pl.BlockSpec((tm, D_MODEL), row, pipeline_mode=single),
                  pl.BlockSpec((PEER_HEADS * 2 * PEER_HALF, D_MODEL), const, pipeline_mode=single),
                  pl.BlockSpec((N_KEYS, PEER_HALF), const, pipeline_mode=single),
                  pl.BlockSpec((N_KEYS, PEER_HALF), const, pipeline_mode=single),
                  pl.BlockSpec((ec, D_MODEL), lambda i, c: (jnp.minimum(c, nc - 1), 0)),
                  pl.BlockSpec((D_MODEL, ec), lambda i, c: (0, jnp.clip(c - 2, 0, nc - 1))),
                  pl.BlockSpec((1, D_MODEL), const, pipeline_mode=single),
                  pl.BlockSpec((1, D_MODEL), const, pipeline_mode=single)],
        out_specs=pl.BlockSpec((tm, D_MODEL), row, pipeline_mode=single),
        out_shape=jax.ShapeDtypeStruct((t, D_MODEL), F32),
        scratch_shapes=[pltpu.VMEM((D_MODEL, tm), BF16),
                        pltpu.VMEM((PEER_HEADS, tm // LANE, N_KEYS, LANE), F32), stat,
                        pltpu.VMEM((PEER_HEADS, 1, tm), F32),
                        pltpu.VMEM((_CAND_ROWS, tm), F32),
                        pltpu.VMEM((2, ec, tm), F32), pltpu.VMEM((2, ec, tm), BF16),
                        pltpu.VMEM((D_MODEL, tm), F32)],
        compiler_params=pltpu.CompilerParams(
            dimension_semantics=("arbitrary", "arbitrary"), vmem_limit_bytes=VMEM_LIMIT_BYTES),
        name="peer",
    )(x1, r, wqt, k1, k2, u, vt, l2g.reshape(1, -1), l2b.reshape(1, -1))


def kernel(x, p, ln_in_g, ln_in_b, w_in, pool_grp_w, pool_scale, lambda_q1, lambda_k1,
           lambda_q2, lambda_k2, subln_g, w_branch_pool, w_branch_attn, gate_b, w_out,
           ln1_g, ln1_b, peer_wq, peer_keys1, peer_keys2, peer_u, peer_v, ple_w, ple_gate_w,
           ln2_g, ln2_b, rel_bias):
    batch, seq, d = x.shape
    assert d == D_MODEL and w_in.shape[0] == DEPTH == 1
    t = batch * seq
    c0, c1 = POOL_WIDTH, POOL_WIDTH + QK_WIDTH
    c2 = c1 + QK_WIDTH
    c3 = c2 + ATTN_WIDTH

    w = w_in[0]
    wq = w[:, c0:c1] * (HEAD_DIM ** -0.5 * LOG2E)
    w_nat = jnp.concatenate([w[:, c3:], w[:, :c0], wq, w[:, c1:c2]], axis=1).astype(BF16)
    w_vt = w[:, c2:c3].T.astype(BF16)

    bias, lam = _attn_bias(rel_bias, lambda_q1[0], lambda_k1[0], lambda_q2[0], lambda_k2[0])
    hn, x0, vt = _in_proj(x.reshape(t, d), ln_in_g, ln_in_b, w_nat, w_vt, batch, seq)
    ya = _diff_attn(lam, hn, vt, bias, subln_g[0], batch, seq)
    x1, r = _merge(hn, ya, x0, p[0].reshape(t, PLE_DIM),
                   pool_grp_w[0].astype(BF16), pool_scale[0].reshape(1, -1),
                   w_branch_pool[0].astype(BF16), w_branch_attn[0].astype(BF16),
                   gate_b[0].reshape(1, -1), w_out[0].astype(BF16),
                   ln1_g[0].reshape(1, -1), ln1_b[0].reshape(1, -1),
                   ple_gate_w[0].astype(BF16), ple_w[0].astype(BF16), seq)
    out = _peer(x1, r, peer_wq[0].T.astype(BF16), peer_keys1[0].astype(BF16),
                peer_keys2[0].astype(BF16), peer_u[0].astype(BF16), peer_v[0].T.astype(BF16),
                ln2_g[0], ln2_b[0])
    return out.reshape(batch, seq, d)
```

```python
import functools
import math

import jax
import jax.numpy as jnp
import numpy as np
from jax import lax
from jax.experimental import pallas as pl
from jax.experimental.pallas import tpu as pltpu

F32 = jnp.float32
BF16 = jnp.bfloat16

D_MODEL = 2048
DEPTH = 1
CHUNK = 64
POOL_WINDOWS = (2, 4, 8, 16)
POOL_GROUPS = 4
POOL_WIDTH = D_MODEL // 2
POOL_GROUP_DIM = POOL_WIDTH // POOL_GROUPS
N_HEADS = 8
HEAD_DIM = D_MODEL // 32
V_DIM = 2 * HEAD_DIM
QK_WIDTH = N_HEADS * 2 * HEAD_DIM
ATTN_WIDTH = N_HEADS * V_DIM
N_BRANCHES = 2
GATE_WIDTH = N_BRANCHES * D_MODEL
NUM_BUCKETS = 32
MAX_DISTANCE = 128
PEER_HEADS = 8
N_KEYS = 128
N_EXPERTS = N_KEYS * N_KEYS
PEER_HALF = 128
PEER_TOPK = 16
PLE_DIM = 256
ALPHA = (2 * DEPTH) ** 0.25
LN_EPS = 1e-5
MASK_VALUE = -1e30
LAM_INIT = 0.8 - 0.6 * math.exp(-0.3 * 0)

LANE = 128
SUBLANE = 8
VMEM_LIMIT_BYTES = 60 * 1024 * 1024

V_AUG = V_DIM + 16
LOG2E = math.log2(math.e)
TQ = 256
TKV = 256
HALO = 16
TM_IN = 512
TN_IN = 1024
TM_MERGE = 256
TM_PEER = 512
EC_PEER = 512

COL_G = 0
COL_P = GATE_WIDTH
COL_Q = COL_P + POOL_WIDTH
COL_K = COL_Q + QK_WIDTH
NAT_COLS = COL_K + QK_WIDTH


def _layer_norm(x, g, b):
    mu = jnp.mean(x, axis=-1, keepdims=True)
    xc = x - mu
    var = jnp.mean(xc * xc, axis=-1, keepdims=True)
    return xc * lax.rsqrt(var + LN_EPS) * g + b


def _sigmoid(x):
    return 1.0 / (1.0 + jnp.exp(-x))


def _t5_bucket(rel):
    nb = NUM_BUCKETS // 2
    ret = (rel > 0).astype(jnp.int32) * nb
    n = jnp.abs(rel)
    max_exact = nb // 2
    is_small = n < max_exact
    nf = jnp.maximum(n, 1).astype(F32)
    large = max_exact + (jnp.log(nf / max_exact) / math.log(MAX_DISTANCE / max_exact)
                         * (nb - max_exact)).astype(jnp.int32)
    large = jnp.minimum(large, nb - 1)
    return ret + jnp.where(is_small, n, large)


def _bias_kernel(tab_ref, lq1_ref, lk1_ref, lq2_ref, lk2_ref, bias_ref, lam_ref):
    h = pl.program_id(0)
    shape = (2 * TKV, 2 * TQ)
    c = lax.broadcasted_iota(jnp.int32, shape, 0)
    r = lax.broadcasted_iota(jnp.int32, shape, 1)
    r = jnp.where(r >= TQ, r - TQ, r)
    rel = c - TKV - r
    bucket = _t5_bucket(rel)
    far = tab_ref[h, NUM_BUCKETS // 2 - 1]
    val = jnp.zeros(shape, F32)
    for b in range(NUM_BUCKETS):
        val = jnp.where(bucket == b, (tab_ref[h, b] - far) * LOG2E, val)
    shift = CHUNK.bit_length() - 1
    allowed = lax.shift_right_logical(c, shift) <= lax.shift_right_logical(r, shift) + TKV // CHUNK
    bias_ref[0:TKV, :] = jnp.zeros((TKV, 2 * TQ), F32)
    bias_ref[TKV:3 * TKV, :] = jnp.where(allowed, val, MASK_VALUE)
    lam = (jnp.exp(jnp.sum(lq1_ref[...] * lk1_ref[...], axis=-1, keepdims=True))
           - jnp.exp(jnp.sum(lq2_ref[...] * lk2_ref[...], axis=-1, keepdims=True)) + LAM_INIT)
    lam_ref[...] = jnp.broadcast_to(lam, lam_ref.shape)


def _attn_bias(rel_bias, lq1, lk1, lq2, lk2):
    assert TQ == TKV and TQ > MAX_DISTANCE and TQ % CHUNK == 0
    tab = rel_bias.T.astype(F32)
    vec = pl.BlockSpec((1, HEAD_DIM), lambda h: (0, 0))
    return pl.pallas_call(
        _bias_kernel,
        grid=(N_HEADS,),
        in_specs=[pl.BlockSpec(memory_space=pltpu.SMEM), vec, vec, vec, vec],
        out_specs=[pl.BlockSpec((None, 3 * TKV, 2 * TQ), lambda h: (h, 0, 0)),
                   pl.BlockSpec((None, SUBLANE, LANE), lambda h: (h, 0, 0))],
        out_shape=[jax.ShapeDtypeStruct((N_HEADS, 3 * TKV, 2 * TQ), F32),
                   jax.ShapeDtypeStruct((N_HEADS, SUBLANE, LANE), F32)],
        name="attn_bias",
    )(tab, lq1.reshape(1, -1), lk1.reshape(1, -1), lq2.reshape(1, -1), lk2.reshape(1, -1))


def _inproj_kernel(x_ref, g_ref, b_ref, w_ref, wvt_ref, hn_ref, x0_ref, vt_ref, xb_scr):
    j = pl.program_id(1)

    @pl.when(j == 0)
    def _():
        y = _layer_norm(x_ref[...], g_ref[...], b_ref[...])
        x0_ref[...] = y
        yb = y.astype(BF16)
        xb_scr[...] = yb
        vt = lax.dot_general(wvt_ref[...], yb, (((1,), (1,)), ((), ())),
                             preferred_element_type=F32)
        row = lax.broadcasted_iota(jnp.int32, (V_AUG - V_DIM, TKV), 0)
        ones_row = jnp.where(row == 0, 1.0, 0.0).astype(BF16)
        for s in range(vt_ref.shape[0]):
            for h in range(N_HEADS):
                vt_ref[s, h, 0:V_DIM, :] = vt[h * V_DIM:(h + 1) * V_DIM,
                                              s * TKV:(s + 1) * TKV].astype(BF16)
                vt_ref[s, h, V_DIM:V_AUG, :] = ones_row

    hn_ref[...] = jnp.dot(xb_scr[...], w_ref[...], preferred_element_type=F32).astype(BF16)


def _in_proj(x2, ln_g, ln_b, w_nat, w_vt, batch, seq):
    t = x2.shape[0]
    tm, tn = TM_IN, TN_IN
    assert seq % tm == 0 and tm % TKV == 0 and NAT_COLS % tn == 0
    nblk = seq // tm
    const = lambda i, j: (0, 0)
    return pl.pallas_call(
        _inproj_kernel,
        grid=(t // tm, NAT_COLS // tn),
        in_specs=[pl.BlockSpec((tm, D_MODEL), lambda i, j: (i, 0)),
                  pl.BlockSpec((1, D_MODEL), const),
                  pl.BlockSpec((1, D_MODEL), const),
                  pl.BlockSpec((None, D_MODEL, tn), lambda i, j: (j, 0, 0)),
                  pl.BlockSpec((ATTN_WIDTH, D_MODEL), const)],
        out_specs=[pl.BlockSpec((tm, tn), lambda i, j: (i, j)),
                   pl.BlockSpec((tm, D_MODEL), lambda i, j: (i, 0)),
                   pl.BlockSpec((None, tm // TKV, N_HEADS, V_AUG, TKV),
                                lambda i, j: (i // nblk, i % nblk, 0, 0, 0))],
        out_shape=[jax.ShapeDtypeStruct((t, NAT_COLS), BF16),
                   jax.ShapeDtypeStruct((t, D_MODEL), F32),
                   jax.ShapeDtypeStruct((batch, seq // TKV, N_HEADS, V_AUG, TKV), BF16)],
        scratch_shapes=[pltpu.VMEM((tm, D_MODEL), BF16)],
        compiler_params=pltpu.CompilerParams(
            dimension_semantics=("arbitrary", "arbitrary"), vmem_limit_bytes=VMEM_LIMIT_BYTES),
        name="in_proj",
    )(x2, ln_g.reshape(1, -1), ln_b.reshape(1, -1), w_nat, w_vt)


def _attn_kernel(lam_ref, q_ref, k_ref, vt_ref, bias_ref, g_ref, o_ref,
                 q2_scr, s_scr, p_scr, al_scr, m_scr, acc_scr):
    i = pl.program_id(2)
    n = i + 1
    qs = q_ref[...].astype(F32)
    lane = lax.broadcasted_iota(jnp.int32, qs.shape, 1)
    q2_scr[...] = jnp.concatenate([jnp.where(lane < HEAD_DIM, qs, 0.0),
                                   jnp.where(lane >= HEAD_DIM, qs, 0.0)], axis=0).astype(BF16)

    m_scr[...] = jnp.full(m_scr.shape, -jnp.inf, F32)
    acc_scr[...] = jnp.zeros(acc_scr.shape, F32)
    s_scr[1] = jnp.zeros(s_scr.shape[1:], F32)
    p_scr[0] = jnp.zeros(p_scr.shape[1:], BF16)
    al_scr[0] = jnp.ones(al_scr.shape[1:], F32)

    def substep(t, sa):
        sb = 1 - sa
        pv = jnp.dot(vt_ref[jnp.clip(t - 2, 0, n - 1)], p_scr[sa],
                     preferred_element_type=F32)
        acc_scr[...] = acc_scr[...] * al_scr[sa] + pv
        live_b = jnp.logical_and(t >= 1, t <= n)
        st = s_scr[sb]
        m_prev = m_scr[...]
        m_new = jnp.where(live_b, jnp.maximum(m_prev, jnp.max(st, axis=0, keepdims=True)), m_prev)
        al_scr[sb] = jnp.where(live_b, jnp.exp2(m_prev - m_new), 1.0)
        m_scr[...] = m_new
        p_scr[sb] = jnp.exp2(st - jnp.where(live_b, m_new, jnp.inf)).astype(BF16)
        ja = jnp.minimum(t, n - 1)
        kblk = k_ref[pl.ds(pl.multiple_of(ja * TKV, TKV), TKV), :]
        boff = jnp.clip(ja - (i - 2), 0, 2) * TKV
        s_scr[sa] = lax.dot_general(kblk, q2_scr[...], (((1,), (1,)), ((), ())),
                                    preferred_element_type=F32) + bias_ref[pl.ds(pl.multiple_of(boff, TKV), TKV), :]

    def body(tt, carry):
        substep(2 * tt, 0)
        substep(2 * tt + 1, 1)
        return carry

    lax.fori_loop(0, (n + 3) // 2, body, 0)

    o = acc_scr[0:V_DIM, :] / acc_scr[V_DIM:V_DIM + 1, :]
    lam = lam_ref[0:1, 0:1]
    att = o[:, :TQ] - lam * o[:, TQ:]
    ms = jnp.mean(att * att, axis=0, keepdims=True)
    y = att * lax.rsqrt(ms + LN_EPS) * g_ref[...] * (1.0 - LAM_INIT)
    o_ref[...] = y.T.astype(BF16)


def _diff_attn(lam, hn, vt, bias, subln_g, batch, seq):
    t = hn.shape[0]
    nq = seq // TQ
    qb, kb = COL_Q // LANE, COL_K // LANE
    return pl.pallas_call(
        _attn_kernel,
        grid=(batch, N_HEADS, nq),
        in_specs=[pl.BlockSpec((None, SUBLANE, LANE), lambda b, h, i: (h, 0, 0)),
                  pl.BlockSpec((TQ, LANE), lambda b, h, i: (b * nq + i, qb + h)),
                  pl.BlockSpec((seq, LANE), lambda b, h, i: (b, kb + h)),
                  pl.BlockSpec((None, seq // TKV, None, V_AUG, TKV), lambda b, h, i: (b, 0, h, 0, 0)),
                  pl.BlockSpec((None, 3 * TKV, 2 * TQ), lambda b, h, i: (h, 0, 0)),
                  pl.BlockSpec((V_DIM, 1), lambda b, h, i: (0, 0))],
        out_specs=pl.BlockSpec((TQ, V_DIM), lambda b, h, i: (b * nq + i, h)),
        out_shape=jax.ShapeDtypeStruct((t, ATTN_WIDTH), BF16),
        scratch_shapes=[pltpu.VMEM((2 * TQ, LANE), BF16),
                        pltpu.VMEM((2, TKV, 2 * TQ), F32), pltpu.VMEM((2, TKV, 2 * TQ), BF16),
                        pltpu.VMEM((2, 1, 2 * TQ), F32),
                        pltpu.VMEM((1, 2 * TQ), F32),
                        pltpu.VMEM((V_AUG, 2 * TQ), F32)],
        compiler_params=pltpu.CompilerParams(
            dimension_semantics=("arbitrary", "arbitrary", "arbitrary"),
            vmem_limit_bytes=VMEM_LIMIT_BYTES),
        name="diff_attn",
    )(lam, hn, hn, vt, bias, subln_g.reshape(-1, 1))


def _merge_kernel(gp_ref, xp_ref, halo_ref, ya_ref, x0_ref, p_ref, wgrp_ref, psc_ref, wbp_ref,
                  wba_ref, gb_ref, wout_ref, l1g_ref, l1b_ref, wg_ref, wp_ref,
                  x1_ref, r_ref, *, nblk):
    i = pl.program_id(0)
    tm = xp_ref.shape[0]
    blk_in_seq = i % nblk
    xp = xp_ref[...].astype(F32)
    halo = halo_ref[...].astype(F32)
    halo = jnp.where(blk_in_seq == 0, jnp.zeros_like(halo), halo)
    ext = jnp.concatenate([halo, xp], axis=0)
    pos = blk_in_seq * tm + lax.broadcasted_iota(jnp.int32, (tm, 1), 0)

    yps = []
    for gi, w in enumerate(POOL_WINDOWS):
        lo, hi = gi * POOL_GROUP_DIM, (gi + 1) * POOL_GROUP_DIM
        s = ext[:, lo:hi]
        span = 1
        while span < w:
            s = s[span:] + s[:-span]
            span *= 2
        start = HALO - (w - 1)
        win = s[start:start + tm]
        cnt = jnp.minimum(pos + 1, w).astype(F32)
        mixed = win / cnt - xp[:, lo:hi]
        yps.append(jnp.dot(mixed.astype(BF16), wgrp_ref[gi], preferred_element_type=F32))
    yp = jnp.concatenate(yps, axis=1) * psc_ref[...]

    gates = _sigmoid(gp_ref[...].astype(F32) + gb_ref[...])
    bp = jnp.dot(yp.astype(BF16), wbp_ref[...], preferred_element_type=F32)
    ba = jnp.dot(ya_ref[...], wba_ref[...], preferred_element_type=F32)
    merged = gates[:, :D_MODEL] * bp + gates[:, D_MODEL:] * ba
    y = jnp.dot(merged.astype(BF16), wout_ref[...], preferred_element_type=F32)
    x1 = _layer_norm(ALPHA * x0_ref[...] + y, l1g_ref[...], l1b_ref[...])
    x1_ref[...] = x1

    x1b = x1.astype(BF16)
    e = (_sigmoid(jnp.dot(x1b, wg_ref[...], preferred_element_type=F32))
         * jnp.dot(p_ref[...].astype(BF16), wp_ref[...], preferred_element_type=F32))
    r_ref[...] = ALPHA * x1 + e


def _merge(hn, ya, x0, p2, wgrp, psc, wbp, wba, gate_b, wout, l1g, l1b, wg, wp, seq):
    t = hn.shape[0]
    tm = TM_MERGE
    assert seq % tm == 0 and tm % HALO == 0
    nblk = seq // tm
    const2 = lambda i: (0, 0)
    single = pl.Buffered(1)
    wspec = lambda shape: pl.BlockSpec(shape, (lambda i: (0,) * len(shape)), pipeline_mode=single)
    return pl.pallas_call(
        functools.partial(_merge_kernel, nblk=nblk),
        grid=(t // tm,),
        in_specs=[pl.BlockSpec((tm, GATE_WIDTH), lambda i: (i, COL_G // GATE_WIDTH)),
                  pl.BlockSpec((tm, POOL_WIDTH), lambda i: (i, COL_P // POOL_WIDTH)),
                  pl.BlockSpec((HALO, POOL_WIDTH),
                               lambda i: (jnp.maximum(i * (tm // HALO) - 1, 0), COL_P // POOL_WIDTH)),
                  pl.BlockSpec((tm, ATTN_WIDTH), lambda i: (i, 0)),
                  pl.BlockSpec((tm, D_MODEL), lambda i: (i, 0)),
                  pl.BlockSpec((tm, PLE_DIM), lambda i: (i, 0)),
                  wspec((POOL_GROUPS, POOL_GROUP_DIM, POOL_GROUP_DIM)),
                  wspec((1, POOL_WIDTH)),
                  wspec((POOL_WIDTH, D_MODEL)),
                  wspec((ATTN_WIDTH, D_MODEL)),
                  wspec((1, GATE_WIDTH)),
                  wspec((D_MODEL, D_MODEL)),
                  wspec((1, D_MODEL)),
                  wspec((1, D_MODEL)),
                  wspec((D_MODEL, D_MODEL)),
                  wspec((PLE_DIM, D_MODEL))],
        out_specs=[pl.BlockSpec((tm, D_MODEL), lambda i: (i, 0)),
                   pl.BlockSpec((tm, D_MODEL), lambda i: (i, 0))],
        out_shape=[jax.ShapeDtypeStruct((t, D_MODEL), F32),
                   jax.ShapeDtypeStruct((t, D_MODEL), F32)],
        compiler_params=pltpu.CompilerParams(
            dimension_semantics=("arbitrary",), vmem_limit_bytes=VMEM_LIMIT_BYTES),
        name="merge",
    )(hn, hn, hn, ya, x0, p2, wgrp, psc, wbp, wba, gate_b, wout, l1g, l1b, wg, wp)


_NTOP = PEER_TOPK + 1
_CAND = [(a, b) for a in range(_NTOP) for b in range(_NTOP)
         if (a + 1) * (b + 1) <= _NTOP]
_CAND_ROWS = -(-len(_CAND) // SUBLANE) * SUBLANE


def _top_values(s, k):
    vals = []
    for _ in range(k):
        m = jnp.max(s, axis=0, keepdims=True)
        vals.append(m)
        s = jnp.where(s == m, -jnp.inf, s)
    return vals


def _peer_kernel(x1_ref, r_ref, wqt_ref, k1_ref, k2_ref, u_ref, vt_ref, g_ref, b_ref, o_ref,
                 xt_scr, thr_scr, s2_scr, kap_scr, cand_scr, h_scr, a_scr, acc_scr, *, nc):
    c = pl.program_id(1)
    tm = x1_ref.shape[0]
    ec = u_ref.shape[0]

    @pl.when(c == 0)
    def _():
        xt = x1_ref[...].T.astype(BF16)
        xt_scr[...] = xt
        cand_scr[...] = jnp.full(cand_scr.shape, -jnp.inf, F32)
        qt = jnp.dot(wqt_ref[...], xt, preferred_element_type=F32)
        for h in range(PEER_HEADS):
            base = h * 2 * PEER_HALF
            s1 = jnp.dot(k1_ref[...], qt[base:base + PEER_HALF].astype(BF16),
                         preferred_element_type=F32)
            s2 = jnp.dot(k2_ref[...], qt[base + PEER_HALF:base + 2 * PEER_HALF].astype(BF16),
                         preferred_element_type=F32)
            v1 = _top_values(s1, _NTOP)
            v2 = _top_values(s2, _NTOP)
            for n, (a, b) in enumerate(_CAND):
                cand_scr[n:n + 1, :] = v1[a] + v2[b]
            cand = cand_scr[...]
            top = _top_values(cand, _NTOP)
            tau = 0.5 * (top[PEER_TOPK - 1] + top[PEER_TOPK])
            cmax = top[0]
            z = jnp.sum(jnp.where(cand >= tau, jnp.exp(cand - cmax), 0.0), axis=0, keepdims=True)
            logz = cmax + jnp.log(z)
            thr = (logz - s1) * LOG2E + 1.0
            for lt in range(tm // LANE):
                thr_scr[h, lt] = thr[:, lt * LANE:(lt + 1) * LANE]
            s2_scr[h] = s2 * LOG2E
            kap_scr[h] = (tau - logz) * LOG2E - 1.0
        acc_scr[...] = jnp.zeros(acc_scr.shape, F32)
        h_scr[1] = jnp.zeros(h_scr.shape[1:], F32)
        a_scr[0] = jnp.zeros(a_scr.shape[1:], BF16)

    slot = c % 2
    cb = jnp.clip(c - 1, 0, nc - 1)
    nal = ec // N_KEYS
    kb_rows = N_KEYS // 2
    acc_scr[...] += jnp.dot(vt_ref[...], a_scr[slot], preferred_element_type=F32)
    for lt in range(tm // LANE):
        cols = slice(lt * LANE, (lt + 1) * LANE)
        for half in range(N_KEYS // kb_rows):
            rows = slice(half * kb_rows, (half + 1) * kb_rows)
            w = [jnp.zeros((kb_rows, LANE), F32) for _ in range(nal)]
            for h in range(PEER_HEADS):
                s2t = s2_scr[h, rows, cols]
                kap = kap_scr[h, :, cols]
                for al in range(nal):
                    thr = thr_scr[h, lt, pl.ds(cb * nal + al, 1), :]
                    x = s2t - thr
                    w[al] = w[al] + jnp.where(x >= kap, jnp.exp2(x), 0.0)
            for al in range(nal):
                r0 = al * N_KEYS + half * kb_rows
                hv = h_scr[1 - slot, r0:r0 + kb_rows, cols]
                a = hv * (1.0 + lax.erf(hv * np.float32(math.sqrt(0.5)))) * w[al]
                a_scr[1 - slot, r0:r0 + kb_rows, cols] = a.astype(BF16)
    h_scr[slot] = jnp.dot(u_ref[...], xt_scr[...], preferred_element_type=F32)


    @pl.when(c == pl.num_programs(1) - 1)
    def _():
        y = r_ref[...] + acc_scr[...].T
        o_ref[...] = _layer_norm(y, g_ref[...], b_ref[...])


def _peer(x1, r, wqt, k1, k2, u, vt, l2g, l2b):
    t = x1.shape[0]
    tm, ec = TM_PEER, EC_PEER
    assert t % tm == 0 and N_EXPERTS % ec == 0 and ec % N_KEYS == 0
    nc = N_EXPERTS // ec
    single = pl.Buffered(1)
    row = lambda i, c: (i, 0)
    const = lambda i, c: (0, 0)
    stat = pltpu.VMEM((PEER_HEADS, N_KEYS, tm), F32)
    return pl.pallas_call(
        functools.partial(_peer_kernel, nc=nc),
        grid=(t // tm, nc + 2),
        in_specs=[pl.BlockSpec((tm, D_MODEL), row, pipeline_mode=single),
                  pl.BlockSpec((tm, D_MODEL), row, pipeline_mode=single),
                  pl.BlockSpec((PEER_HEADS * 2 * PEER_HALF, D_MODEL), const, pipeline_mode=single),
                  pl.BlockSpec((N_KEYS, PEER_HALF), const, pipeline_mode=single),
                  pl.BlockSpec((N_KEYS, PEER_HALF), const, pipeline_mode=single),
                  pl.BlockSpec((ec, D_MODEL), lambda i, c: (jnp.minimum(c, nc - 1), 0)),
                  pl.BlockSpec((None, D_MODEL, ec), lambda i, c: (jnp.clip(c - 2, 0, nc - 1), 0, 0)),
                  pl.BlockSpec((1, D_MODEL), const, pipeline_mode=single),
                  pl.BlockSpec((1, D_MODEL), const, pipeline_mode=single)],
        out_specs=pl.BlockSpec((tm, D_MODEL), row, pipeline_mode=single),
        out_shape=jax.ShapeDtypeStruct((t, D_MODEL), F32),
        scratch_shapes=[pltpu.VMEM((D_MODEL, tm), BF16),
                        pltpu.VMEM((PEER_HEADS, tm // LANE, N_KEYS, LANE), F32), stat,
                        pltpu.VMEM((PEER_HEADS, 1, tm), F32),
                        pltpu.VMEM((_CAND_ROWS, tm), F32),
                        pltpu.VMEM((2, ec, tm), F32), pltpu.VMEM((2, ec, tm), BF16),
                        pltpu.VMEM((D_MODEL, tm), F32)],
        compiler_params=pltpu.CompilerParams(
            dimension_semantics=("arbitrary", "arbitrary"), vmem_limit_bytes=VMEM_LIMIT_BYTES,
        ),
        name="peer",
    )(x1, r, wqt, k1, k2, u, vt, l2g.reshape(1, -1), l2b.reshape(1, -1))


def kernel(x, p, ln_in_g, ln_in_b, w_in, pool_grp_w, pool_scale, lambda_q1, lambda_k1,
           lambda_q2, lambda_k2, subln_g, w_branch_pool, w_branch_attn, gate_b, w_out,
           ln1_g, ln1_b, peer_wq, peer_keys1, peer_keys2, peer_u, peer_v, ple_w, ple_gate_w,
           ln2_g, ln2_b, rel_bias):
    batch, seq, d = x.shape
    assert d == D_MODEL and w_in.shape[0] == DEPTH == 1
    t = batch * seq
    c0, c1 = POOL_WIDTH, POOL_WIDTH + QK_WIDTH
    c2 = c1 + QK_WIDTH
    c3 = c2 + ATTN_WIDTH

    w = w_in[0]
    wq = w[:, c0:c1] * (HEAD_DIM ** -0.5 * LOG2E)
    w_nat = jnp.concatenate([w[:, c3:], w[:, :c0], wq, w[:, c1:c2]], axis=1).astype(BF16)
    w_nat = w_nat.reshape(D_MODEL, NAT_COLS // TN_IN, TN_IN).transpose(1, 0, 2)
    w_vt = w[:, c2:c3].T.astype(BF16)

    bias, lam = _attn_bias(rel_bias, lambda_q1[0], lambda_k1[0], lambda_q2[0], lambda_k2[0])
    hn, x0, vt = _in_proj(x.reshape(t, d), ln_in_g, ln_in_b, w_nat, w_vt, batch, seq)
    ya = _diff_attn(lam, hn, vt, bias, subln_g[0], batch, seq)
    x1, r = _merge(hn, ya, x0, p[0].reshape(t, PLE_DIM),
                   pool_grp_w[0].astype(BF16), pool_scale[0].reshape(1, -1),
                   w_branch_pool[0].astype(BF16), w_branch_attn[0].astype(BF16),
                   gate_b[0].reshape(1, -1), w_out[0].astype(BF16),
                   ln1_g[0].reshape(1, -1), ln1_b[0].reshape(1, -1),
                   ple_gate_w[0].astype(BF16), ple_w[0].astype(BF16), seq)
    out = _peer(x1, r, peer_wq[0].T.astype(BF16), peer_keys1[0].astype(BF16),
                peer_keys2[0].astype(BF16), peer_u[0].astype(BF16),
                peer_v[0].astype(BF16).reshape(N_EXPERTS // EC_PEER, EC_PEER, d).transpose(0, 2, 1),
                ln2_g[0], ln2_b[0])
    return out.reshape(batch, seq, d)
```

```python
import functools
import math

import jax
import jax.numpy as jnp
import numpy as np
from jax import lax
from jax.experimental import pallas as pl
from jax.experimental.pallas import tpu as pltpu

F32 = jnp.float32
BF16 = jnp.bfloat16

D_MODEL = 2048
DEPTH = 1
CHUNK = 64
POOL_WINDOWS = (2, 4, 8, 16)
POOL_GROUPS = 4
POOL_WIDTH = D_MODEL // 2
POOL_GROUP_DIM = POOL_WIDTH // POOL_GROUPS
N_HEADS = 8
HEAD_DIM = D_MODEL // 32
V_DIM = 2 * HEAD_DIM
QK_WIDTH = N_HEADS * 2 * HEAD_DIM
ATTN_WIDTH = N_HEADS * V_DIM
N_BRANCHES = 2
GATE_WIDTH = N_BRANCHES * D_MODEL
NUM_BUCKETS = 32
MAX_DISTANCE = 128
PEER_HEADS = 8
N_KEYS = 128
N_EXPERTS = N_KEYS * N_KEYS
PEER_HALF = 128
PEER_TOPK = 16
PLE_DIM = 256
ALPHA = (2 * DEPTH) ** 0.25
LN_EPS = 1e-5
MASK_VALUE = -1e30
LAM_INIT = 0.8 - 0.6 * math.exp(-0.3 * 0)

LANE = 128
SUBLANE = 8
VMEM_LIMIT_BYTES = 60 * 1024 * 1024

V_AUG = V_DIM + 16
LOG2E = math.log2(math.e)
TQ = 256
TKV = 256
HALO = 16
TM_IN = 512
TN_IN = 1024
TM_MERGE = 256
TM_PEER = 512
EC_PEER = 512

COL_G = 0
COL_P = GATE_WIDTH
COL_Q = COL_P + POOL_WIDTH
COL_K = COL_Q + QK_WIDTH
NAT_COLS = COL_K + QK_WIDTH


def _layer_norm(x, g, b):
    mu = jnp.mean(x, axis=-1, keepdims=True)
    xc = x - mu
    var = jnp.mean(xc * xc, axis=-1, keepdims=True)
    return xc * lax.rsqrt(var + LN_EPS) * g + b


def _sigmoid(x):
    return 1.0 / (1.0 + jnp.exp(-x))


def _t5_bucket(rel):
    nb = NUM_BUCKETS // 2
    ret = (rel > 0).astype(jnp.int32) * nb
    n = jnp.abs(rel)
    max_exact = nb // 2
    is_small = n < max_exact
    nf = jnp.maximum(n, 1).astype(F32)
    large = max_exact + (jnp.log(nf / max_exact) / math.log(MAX_DISTANCE / max_exact)
                         * (nb - max_exact)).astype(jnp.int32)
    large = jnp.minimum(large, nb - 1)
    return ret + jnp.where(is_small, n, large)


def _bias_kernel(tab_ref, lq1_ref, lk1_ref, lq2_ref, lk2_ref, bias_ref, lam_ref):
    h = pl.program_id(0)
    shape = (2 * TKV, 2 * TQ)
    c = lax.broadcasted_iota(jnp.int32, shape, 0)
    r = lax.broadcasted_iota(jnp.int32, shape, 1)
    r = jnp.where(r >= TQ, r - TQ, r)
    rel = c - TKV - r
    bucket = _t5_bucket(rel)
    far = tab_ref[h, NUM_BUCKETS // 2 - 1]
    val = jnp.zeros(shape, F32)
    for b in range(NUM_BUCKETS):
        val = jnp.where(bucket == b, (tab_ref[h, b] - far) * LOG2E, val)
    shift = CHUNK.bit_length() - 1
    allowed = lax.shift_right_logical(c, shift) <= lax.shift_right_logical(r, shift) + TKV // CHUNK
    bias_ref[0:TKV, :] = jnp.zeros((TKV, 2 * TQ), F32)
    bias_ref[TKV:3 * TKV, :] = jnp.where(allowed, val, MASK_VALUE)
    lam = (jnp.exp(jnp.sum(lq1_ref[...] * lk1_ref[...], axis=-1, keepdims=True))
           - jnp.exp(jnp.sum(lq2_ref[...] * lk2_ref[...], axis=-1, keepdims=True)) + LAM_INIT)
    lam_ref[...] = jnp.broadcast_to(lam, lam_ref.shape)


def _attn_bias(rel_bias, lq1, lk1, lq2, lk2):
    assert TQ == TKV and TQ > MAX_DISTANCE and TQ % CHUNK == 0
    tab = rel_bias.T.astype(F32)
    vec = pl.BlockSpec((1, HEAD_DIM), lambda h: (0, 0))
    return pl.pallas_call(
        _bias_kernel,
        grid=(N_HEADS,),
        in_specs=[pl.BlockSpec(memory_space=pltpu.SMEM), vec, vec, vec, vec],
        out_specs=[pl.BlockSpec((None, 3 * TKV, 2 * TQ), lambda h: (h, 0, 0)),
                   pl.BlockSpec((None, SUBLANE, LANE), lambda h: (h, 0, 0))],
        out_shape=[jax.ShapeDtypeStruct((N_HEADS, 3 * TKV, 2 * TQ), F32),
                   jax.ShapeDtypeStruct((N_HEADS, SUBLANE, LANE), F32)],
        name="attn_bias",
    )(tab, lq1.reshape(1, -1), lk1.reshape(1, -1), lq2.reshape(1, -1), lk2.reshape(1, -1))


def _inproj_kernel(x_ref, g_ref, b_ref, w_ref, wvt_ref, hn_ref, x0_ref, vt_ref, xb_scr):
    j = pl.program_id(1)

    @pl.when(j == 0)
    def _():
        y = _layer_norm(x_ref[...], g_ref[...], b_ref[...])
        x0_ref[...] = y
        yb = y.astype(BF16)
        xb_scr[...] = yb
        vt = lax.dot_general(wvt_ref[...], yb, (((1,), (1,)), ((), ())),
                             preferred_element_type=F32)
        row = lax.broadcasted_iota(jnp.int32, (V_AUG - V_DIM, TKV), 0)
        ones_row = jnp.where(row == 0, 1.0, 0.0).astype(BF16)
        for s in range(vt_ref.shape[0]):
            for h in range(N_HEADS):
                vt_ref[s, h, 0:V_DIM, :] = vt[h * V_DIM:(h + 1) * V_DIM,
                                              s * TKV:(s + 1) * TKV].astype(BF16)
                vt_ref[s, h, V_DIM:V_AUG, :] = ones_row

    hn_ref[...] = jnp.dot(xb_scr[...], w_ref[...], preferred_element_type=F32).astype(BF16)


def _in_proj(x2, ln_g, ln_b, w_nat, w_vt, batch, seq):
    t = x2.shape[0]
    tm, tn = TM_IN, TN_IN
    assert seq % tm == 0 and tm % TKV == 0 and NAT_COLS % tn == 0
    nblk = seq // tm
    const = lambda i, j: (0, 0)
    return pl.pallas_call(
        _inproj_kernel,
        grid=(t // tm, NAT_COLS // tn),
        in_specs=[pl.BlockSpec((tm, D_MODEL), lambda i, j: (i, 0)),
                  pl.BlockSpec((1, D_MODEL), const),
                  pl.BlockSpec((1, D_MODEL), const),
                  pl.BlockSpec((None, D_MODEL, tn), lambda i, j: (j, 0, 0)),
                  pl.BlockSpec((ATTN_WIDTH, D_MODEL), const)],
        out_specs=[pl.BlockSpec((tm, tn), lambda i, j: (i, j)),
                   pl.BlockSpec((tm, D_MODEL), lambda i, j: (i, 0)),
                   pl.BlockSpec((None, tm // TKV, N_HEADS, V_AUG, TKV),
                                lambda i, j: (i // nblk, i % nblk, 0, 0, 0))],
        out_shape=[jax.ShapeDtypeStruct((t, NAT_COLS), BF16),
                   jax.ShapeDtypeStruct((t, D_MODEL), F32),
                   jax.ShapeDtypeStruct((batch, seq // TKV, N_HEADS, V_AUG, TKV), BF16)],
        scratch_shapes=[pltpu.VMEM((tm, D_MODEL), BF16)],
        compiler_params=pltpu.CompilerParams(
            dimension_semantics=("arbitrary", "arbitrary"), vmem_limit_bytes=VMEM_LIMIT_BYTES),
        name="in_proj",
    )(x2, ln_g.reshape(1, -1), ln_b.reshape(1, -1), w_nat, w_vt)


def _attn_kernel(lam_ref, q_ref, k_ref, vt_ref, bias_ref, g_ref, o_ref,
                 q2_scr, s_scr, p_scr, al_scr, m_scr, acc_scr, *, nq):
    npairs = nq * (nq + 1) // 2
    lane = lax.broadcasted_iota(jnp.int32, (TQ, LANE), 1)

    def prep(i, carry):
        qs = q_ref[pl.ds(pl.multiple_of(i * TQ, TQ), TQ), :].astype(F32)
        q2_scr[i] = jnp.concatenate([jnp.where(lane < HEAD_DIM, qs, 0.0),
                                     jnp.where(lane >= HEAD_DIM, qs, 0.0)], axis=0).astype(BF16)
        m_scr[i] = jnp.full(m_scr.shape[1:], -jnp.inf, F32)
        acc_scr[i] = jnp.zeros(acc_scr.shape[1:], F32)
        return carry

    lax.fori_loop(0, nq, prep, 0)
    s_scr[1] = jnp.zeros(s_scr.shape[1:], F32)
    p_scr[0] = jnp.zeros(p_scr.shape[1:], BF16)
    al_scr[0] = jnp.ones(al_scr.shape[1:], F32)

    def advance(i, j):
        wrap = j >= i
        last = jnp.logical_and(i == nq - 1, wrap)
        ni = jnp.where(jnp.logical_and(wrap, jnp.logical_not(last)), i + 1, i)
        nj = jnp.where(last, j, jnp.where(wrap, 0, j + 1))
        return ni, nj

    def substep(t, sa, pairs):
        (ia, ja), (ib, jb), (ic, jc) = pairs
        sb = 1 - sa
        pv = jnp.dot(vt_ref[jc], p_scr[sa], preferred_element_type=F32)
        acc_scr[ic] = acc_scr[ic] * al_scr[sa] + pv
        live_b = jnp.logical_and(t >= 1, t <= npairs)
        st = s_scr[sb]
        m_prev = m_scr[ib]
        m_new = jnp.where(live_b, jnp.maximum(m_prev, jnp.max(st, axis=0, keepdims=True)), m_prev)
        al_scr[sb] = jnp.where(live_b, jnp.exp2(m_prev - m_new), 1.0)
        m_scr[ib] = m_new
        p_scr[sb] = jnp.exp2(st - jnp.where(live_b, m_new, jnp.inf)).astype(BF16)
        kblk = k_ref[pl.ds(pl.multiple_of(ja * TKV, TKV), TKV), :]
        boff = jnp.clip(ja - (ia - 2), 0, 2) * TKV
        s_scr[sa] = lax.dot_general(kblk, q2_scr[ia], (((1,), (1,)), ((), ())),
                                    preferred_element_type=F32) + bias_ref[pl.ds(pl.multiple_of(boff, TKV), TKV), :]
        return (advance(ia, ja), (ia, ja), (ib, jb))

    def body(tt, pairs):
        pairs = substep(2 * tt, 0, pairs)
        return substep(2 * tt + 1, 1, pairs)

    zero = jnp.int32(0)
    lax.fori_loop(0, (npairs + 3) // 2, body, ((zero, zero), (zero, zero), (zero, zero)))

    lam = lam_ref[0:1, 0:1]

    def finish(i, carry):
        acc = acc_scr[i]
        o = acc[0:V_DIM, :] / acc[V_DIM:V_DIM + 1, :]
        att = o[:, :TQ] - lam * o[:, TQ:]
        ms = jnp.mean(att * att, axis=0, keepdims=True)
        y = att * lax.rsqrt(ms + LN_EPS) * g_ref[...] * (1.0 - LAM_INIT)
        o_ref[pl.ds(pl.multiple_of(i * TQ, TQ), TQ), :] = y.T.astype(BF16)
        return carry

    lax.fori_loop(0, nq, finish, 0)


def _diff_attn(lam, hn, vt, bias, subln_g, batch, seq):
    t = hn.shape[0]
    nq = seq // TQ
    qb, kb = COL_Q // LANE, COL_K // LANE
    return pl.pallas_call(
        functools.partial(_attn_kernel, nq=nq),
        grid=(batch, N_HEADS),
        in_specs=[pl.BlockSpec((None, SUBLANE, LANE), lambda b, h: (h, 0, 0)),
                  pl.BlockSpec((seq, LANE), lambda b, h: (b, qb + h)),
                  pl.BlockSpec((seq, LANE), lambda b, h: (b, kb + h)),
                  pl.BlockSpec((None, seq // TKV, None, V_AUG, TKV), lambda b, h: (b, 0, h, 0, 0)),
                  pl.BlockSpec((None, 3 * TKV, 2 * TQ), lambda b, h: (h, 0, 0)),
                  pl.BlockSpec((V_DIM, 1), lambda b, h: (0, 0))],
        out_specs=pl.BlockSpec((seq, V_DIM), lambda b, h: (b, h)),
        out_shape=jax.ShapeDtypeStruct((t, ATTN_WIDTH), BF16),
        scratch_shapes=[pltpu.VMEM((nq, 2 * TQ, LANE), BF16),
                        pltpu.VMEM((2, TKV, 2 * TQ), F32), pltpu.VMEM((2, TKV, 2 * TQ), BF16),
                        pltpu.VMEM((2, 1, 2 * TQ), F32),
                        pltpu.VMEM((nq, 1, 2 * TQ), F32),
                        pltpu.VMEM((nq, V_AUG, 2 * TQ), F32)],
        compiler_params=pltpu.CompilerParams(
            dimension_semantics=("arbitrary", "arbitrary"),
            vmem_limit_bytes=VMEM_LIMIT_BYTES),
        name="diff_attn",
    )(lam, hn, hn, vt, bias, subln_g.reshape(-1, 1))


def _merge_kernel(gp_ref, xp_ref, halo_ref, ya_ref, x0_ref, p_ref, wgrp_ref, psc_ref, wbp_ref,
                  wba_ref, gb_ref, wout_ref, l1g_ref, l1b_ref, wg_ref, wp_ref,
                  x1_ref, r_ref, *, nblk):
    i = pl.program_id(0)
    tm = xp_ref.shape[0]
    blk_in_seq = i % nblk
    xp = xp_ref[...].astype(F32)
    halo = halo_ref[...].astype(F32)
    halo = jnp.where(blk_in_seq == 0, jnp.zeros_like(halo), halo)
    ext = jnp.concatenate([halo, xp], axis=0)
    pos = blk_in_seq * tm + lax.broadcasted_iota(jnp.int32, (tm, 1), 0)

    yps = []
    for gi, w in enumerate(POOL_WINDOWS):
        lo, hi = gi * POOL_GROUP_DIM, (gi + 1) * POOL_GROUP_DIM
        s = ext[:, lo:hi]
        span = 1
        while span < w:
            s = s[span:] + s[:-span]
            span *= 2
        start = HALO - (w - 1)
        win = s[start:start + tm]
        cnt = jnp.minimum(pos + 1, w).astype(F32)
        mixed = win / cnt - xp[:, lo:hi]
        yps.append(jnp.dot(mixed.astype(BF16), wgrp_ref[gi], preferred_element_type=F32))
    yp = jnp.concatenate(yps, axis=1) * psc_ref[...]

    gates = _sigmoid(gp_ref[...].astype(F32) + gb_ref[...])
    bp = jnp.dot(yp.astype(BF16), wbp_ref[...], preferred_element_type=F32)
    ba = jnp.dot(ya_ref[...], wba_ref[...], preferred_element_type=F32)
    merged = gates[:, :D_MODEL] * bp + gates[:, D_MODEL:] * ba
    y = jnp.dot(merged.astype(BF16), wout_ref[...], preferred_element_type=F32)
    x1 = _layer_norm(ALPHA * x0_ref[...] + y, l1g_ref[...], l1b_ref[...])
    x1_ref[...] = x1

    x1b = x1.astype(BF16)
    e = (_sigmoid(jnp.dot(x1b, wg_ref[...], preferred_element_type=F32))
         * jnp.dot(p_ref[...].astype(BF16), wp_ref[...], preferred_element_type=F32))
    r_ref[...] = ALPHA * x1 + e


def _merge(hn, ya, x0, p2, wgrp, psc, wbp, wba, gate_b, wout, l1g, l1b, wg, wp, seq):
    t = hn.shape[0]
    tm = TM_MERGE
    assert seq % tm == 0 and tm % HALO == 0
    nblk = seq // tm
    const2 = lambda i: (0, 0)
    single = pl.Buffered(1)
    wspec = lambda shape: pl.BlockSpec(shape, (lambda i: (0,) * len(shape)), pipeline_mode=single)
    return pl.pallas_call(
        functools.partial(_merge_kernel, nblk=nblk),
        grid=(t // tm,),
        in_specs=[pl.BlockSpec((tm, GATE_WIDTH), lambda i: (i, COL_G // GATE_WIDTH)),
                  pl.BlockSpec((tm, POOL_WIDTH), lambda i: (i, COL_P // POOL_WIDTH)),
                  pl.BlockSpec((HALO, POOL_WIDTH),
                               lambda i: (jnp.maximum(i * (tm // HALO) - 1, 0), COL_P // POOL_WIDTH)),
                  pl.BlockSpec((tm, ATTN_WIDTH), lambda i: (i, 0)),
                  pl.BlockSpec((tm, D_MODEL), lambda i: (i, 0)),
                  pl.BlockSpec((tm, PLE_DIM), lambda i: (i, 0)),
                  wspec((POOL_GROUPS, POOL_GROUP_DIM, POOL_GROUP_DIM)),
                  wspec((1, POOL_WIDTH)),
                  wspec((POOL_WIDTH, D_MODEL)),
                  wspec((ATTN_WIDTH, D_MODEL)),
                  wspec((1, GATE_WIDTH)),
                  wspec((D_MODEL, D_MODEL)),
                  wspec((1, D_MODEL)),
                  wspec((1, D_MODEL)),
                  wspec((D_MODEL, D_MODEL)),
                  wspec((PLE_DIM, D_MODEL))],
        out_specs=[pl.BlockSpec((tm, D_MODEL), lambda i: (i, 0)),
                   pl.BlockSpec((tm, D_MODEL), lambda i: (i, 0))],
        out_shape=[jax.ShapeDtypeStruct((t, D_MODEL), F32),
                   jax.ShapeDtypeStruct((t, D_MODEL), F32)],
        compiler_params=pltpu.CompilerParams(
            dimension_semantics=("arbitrary",), vmem_limit_bytes=VMEM_LIMIT_BYTES),
        name="merge",
    )(hn, hn, hn, ya, x0, p2, wgrp, psc, wbp, wba, gate_b, wout, l1g, l1b, wg, wp)


_NTOP = PEER_TOPK + 1
_CAND = [(a, b) for a in range(_NTOP) for b in range(_NTOP)
         if (a + 1) * (b + 1) <= _NTOP]
_CAND_ROWS = -(-len(_CAND) // SUBLANE) * SUBLANE


def _top_values(s, k):
    vals = []
    for _ in range(k):
        m = jnp.max(s, axis=0, keepdims=True)
        vals.append(m)
        s = jnp.where(s == m, -jnp.inf, s)
    return vals


def _peer_kernel(x1_ref, r_ref, wqt_ref, k1_ref, k2_ref, u_ref, vt_ref, g_ref, b_ref, o_ref,
                 xt_scr, thr_scr, s2_scr, kap_scr, cand_scr, h_scr, a_scr, acc_scr, *, nc):
    c = pl.program_id(1)
    tm = x1_ref.shape[0]
    ec = u_ref.shape[0]

    @pl.when(c == 0)
    def _():
        xt = x1_ref[...].T.astype(BF16)
        xt_scr[...] = xt
        cand_scr[...] = jnp.full(cand_scr.shape, -jnp.inf, F32)
        qt = jnp.dot(wqt_ref[...], xt, preferred_element_type=F32)
        for h in range(PEER_HEADS):
            base = h * 2 * PEER_HALF
            s1 = jnp.dot(k1_ref[...], qt[base:base + PEER_HALF].astype(BF16),
                         preferred_element_type=F32)
            s2 = jnp.dot(k2_ref[...], qt[base + PEER_HALF:base + 2 * PEER_HALF].astype(BF16),
                         preferred_element_type=F32)
            v1 = _top_values(s1, _NTOP)
            v2 = _top_values(s2, _NTOP)
            for n, (a, b) in enumerate(_CAND):
                cand_scr[n:n + 1, :] = v1[a] + v2[b]
            cand = cand_scr[...]
            top = _top_values(cand, _NTOP)
            tau = 0.5 * (top[PEER_TOPK - 1] + top[PEER_TOPK])
            cmax = top[0]
            z = jnp.sum(jnp.where(cand >= tau, jnp.exp(cand - cmax), 0.0), axis=0, keepdims=True)
            logz = cmax + jnp.log(z)
            thr = (logz - s1) * LOG2E + 1.0
            for lt in range(tm // LANE):
                thr_scr[h, lt] = thr[:, lt * LANE:(lt + 1) * LANE]
            s2_scr[h] = s2 * LOG2E
            kap_scr[h] = (tau - logz) * LOG2E - 1.0
        acc_scr[...] = jnp.zeros(acc_scr.shape, F32)
        h_scr[1] = jnp.zeros(h_scr.shape[1:], F32)
        a_scr[0] = jnp.zeros(a_scr.shape[1:], BF16)

    slot = c % 2
    cb = jnp.clip(c - 1, 0, nc - 1)
    nal = ec // N_KEYS
    kb_rows = N_KEYS // 2
    acc_scr[...] += jnp.dot(vt_ref[...], a_scr[slot], preferred_element_type=F32)
    for lt in range(tm // LANE):
        cols = slice(lt * LANE, (lt + 1) * LANE)
        for half in range(N_KEYS // kb_rows):
            rows = slice(half * kb_rows, (half + 1) * kb_rows)
            w = [jnp.zeros((kb_rows, LANE), F32) for _ in range(nal)]
            for h in range(PEER_HEADS):
                s2t = s2_scr[h, rows, cols]
                kap = kap_scr[h, :, cols]
                for al in range(nal):
                    thr = thr_scr[h, lt, pl.ds(cb * nal + al, 1), :]
                    x = s2t - thr
                    w[al] = w[al] + jnp.where(x >= kap, jnp.exp2(x), 0.0)
            for al in range(nal):
                r0 = al * N_KEYS + half * kb_rows
                hv = h_scr[1 - slot, r0:r0 + kb_rows, cols]
                a = hv * (1.0 + lax.erf(hv * np.float32(math.sqrt(0.5)))) * w[al]
                a_scr[1 - slot, r0:r0 + kb_rows, cols] = a.astype(BF16)
    h_scr[slot] = jnp.dot(u_ref[...], xt_scr[...], preferred_element_type=F32)


    @pl.when(c == pl.num_programs(1) - 1)
    def _():
        y = r_ref[...] + acc_scr[...].T
        o_ref[...] = _layer_norm(y, g_ref[...], b_ref[...])


def _peer(x1, r, wqt, k1, k2, u, vt, l2g, l2b):
    t = x1.shape[0]
    tm, ec = TM_PEER, EC_PEER
    assert t % tm == 0 and N_EXPERTS % ec == 0 and ec % N_KEYS == 0
    nc = N_EXPERTS // ec
    single = pl.Buffered(1)
    row = lambda i, c: (i, 0)
    const = lambda i, c: (0, 0)
    stat = pltpu.VMEM((PEER_HEADS, N_KEYS, tm), F32)
    return pl.pallas_call(
        functools.partial(_peer_kernel, nc=nc),
        grid=(t // tm, nc + 2),
        in_specs=[pl.BlockSpec((tm, D_MODEL), row, pipeline_mode=single),
                  pl.BlockSpec((tm, D_MODEL), row, pipeline_mode=single),
                  pl.BlockSpec((PEER_HEADS * 2 * PEER_HALF, D_MODEL), const, pipeline_mode=single),
                  pl.BlockSpec((N_KEYS, PEER_HALF), const, pipeline_mode=single),
                  pl.BlockSpec((N_KEYS, PEER_HALF), const, pipeline_mode=single),
                  pl.BlockSpec((ec, D_MODEL), lambda i, c: (jnp.minimum(c, nc - 1), 0)),
                  pl.BlockSpec((None, D_MODEL, ec), lambda i, c: (jnp.clip(c - 2, 0, nc - 1), 0, 0)),
                  pl.BlockSpec((1, D_MODEL), const, pipeline_mode=single),
                  pl.BlockSpec((1, D_MODEL), const, pipeline_mode=single)],
        out_specs=pl.BlockSpec((tm, D_MODEL), row, pipeline_mode=single),
        out_shape=jax.ShapeDtypeStruct((t, D_MODEL), F32),
        scratch_shapes=[pltpu.VMEM((D_MODEL, tm), BF16),
                        pltpu.VMEM((PEER_HEADS, tm // LANE, N_KEYS, LANE), F32), stat,
                        pltpu.VMEM((PEER_HEADS, 1, tm), F32),
                        pltpu.VMEM((_CAND_ROWS, tm), F32),
                        pltpu.VMEM((2, ec, tm), F32), pltpu.VMEM((2, ec, tm), BF16),
                        pltpu.VMEM((D_MODEL, tm), F32)],
        compiler_params=pltpu.CompilerParams(
            dimension_semantics=("arbitrary", "arbitrary"), vmem_limit_bytes=VMEM_LIMIT_BYTES,
        ),
        name="peer",
    )(x1, r, wqt, k1, k2, u, vt, l2g.reshape(1, -1), l2b.reshape(1, -1))


def kernel(x, p, ln_in_g, ln_in_b, w_in, pool_grp_w, pool_scale, lambda_q1, lambda_k1,
           lambda_q2, lambda_k2, subln_g, w_branch_pool, w_branch_attn, gate_b, w_out,
           ln1_g, ln1_b, peer_wq, peer_keys1, peer_keys2, peer_u, peer_v, ple_w, ple_gate_w,
           ln2_g, ln2_b, rel_bias):
    batch, seq, d = x.shape
    assert d == D_MODEL and w_in.shape[0] == DEPTH == 1
    t = batch * seq
    c0, c1 = POOL_WIDTH, POOL_WIDTH + QK_WIDTH
    c2 = c1 + QK_WIDTH
    c3 = c2 + ATTN_WIDTH

    w = w_in[0]
    wq = w[:, c0:c1] * (HEAD_DIM ** -0.5 * LOG2E)
    w_nat = jnp.concatenate([w[:, c3:], w[:, :c0], wq, w[:, c1:c2]], axis=1).astype(BF16)
    w_nat = w_nat.reshape(D_MODEL, NAT_COLS // TN_IN, TN_IN).transpose(1, 0, 2)
    w_vt = w[:, c2:c3].T.astype(BF16)

    bias, lam = _attn_bias(rel_bias, lambda_q1[0], lambda_k1[0], lambda_q2[0], lambda_k2[0])
    hn, x0, vt = _in_proj(x.reshape(t, d), ln_in_g, ln_in_b, w_nat, w_vt, batch, seq)
    ya = _diff_attn(lam, hn, vt, bias, subln_g[0], batch, seq)
    x1, r = _merge(hn, ya, x0, p[0].reshape(t, PLE_DIM),
                   pool_grp_w[0].astype(BF16), pool_scale[0].reshape(1, -1),
                   w_branch_pool[0].astype(BF16), w_branch_attn[0].astype(BF16),
                   gate_b[0].reshape(1, -1), w_out[0].astype(BF16),
                   ln1_g[0].reshape(1, -1), ln1_b[0].reshape(1, -1),
                   ple_gate_w[0].astype(BF16), ple_w[0].astype(BF16), seq)
    out = _peer(x1, r, peer_wq[0].T.astype(BF16), peer_keys1[0].astype(BF16),
                peer_keys2[0].astype(BF16), peer_u[0].astype(BF16),
                peer_v[0].astype(BF16).reshape(N_EXPERTS // EC_PEER, EC_PEER, d).transpose(0, 2, 1),
                ln2_g[0], ln2_b[0])
    return out.reshape(batch, seq, d)
```

```python
import functools
import math

import jax
import jax.numpy as jnp
import numpy as np
from jax import lax
from jax.experimental import pallas as pl
from jax.experimental.pallas import tpu as pltpu

F32 = jnp.float32
BF16 = jnp.bfloat16

D_MODEL = 2048
DEPTH = 1
CHUNK = 64
POOL_WINDOWS = (2, 4, 8, 16)
POOL_GROUPS = 4
POOL_WIDTH = D_MODEL // 2
POOL_GROUP_DIM = POOL_WIDTH // POOL_GROUPS
N_HEADS = 8
HEAD_DIM = D_MODEL // 32
V_DIM = 2 * HEAD_DIM
QK_WIDTH = N_HEADS * 2 * HEAD_DIM
ATTN_WIDTH = N_HEADS * V_DIM
N_BRANCHES = 2
GATE_WIDTH = N_BRANCHES * D_MODEL
NUM_BUCKETS = 32
MAX_DISTANCE = 128
PEER_HEADS = 8
N_KEYS = 128
N_EXPERTS = N_KEYS * N_KEYS
PEER_HALF = 128
PEER_TOPK = 16
PLE_DIM = 256
ALPHA = (2 * DEPTH) ** 0.25
LN_EPS = 1e-5
MASK_VALUE = -1e30
LAM_INIT = 0.8 - 0.6 * math.exp(-0.3 * 0)

LANE = 128
SUBLANE = 8
VMEM_LIMIT_BYTES = 60 * 1024 * 1024

V_AUG = V_DIM + 16
LOG2E = math.log2(math.e)
TQ = 256
TKV = 256
HALO = 16
TM_IN = 512
TN_IN = 1024
TM_MERGE = 256
TM_PEER = 512
EC_PEER = 512

COL_G = 0
COL_P = GATE_WIDTH
COL_Q = COL_P + POOL_WIDTH
COL_K = COL_Q + QK_WIDTH
NAT_COLS = COL_K + QK_WIDTH


def _layer_norm(x, g, b):
    mu = jnp.mean(x, axis=-1, keepdims=True)
    xc = x - mu
    var = jnp.mean(xc * xc, axis=-1, keepdims=True)
    return xc * lax.rsqrt(var + LN_EPS) * g + b


def _sigmoid(x):
    return 1.0 / (1.0 + jnp.exp(-x))


def _t5_bucket(rel):
    nb = NUM_BUCKETS // 2
    ret = (rel > 0).astype(jnp.int32) * nb
    n = jnp.abs(rel)
    max_exact = nb // 2
    is_small = n < max_exact
    nf = jnp.maximum(n, 1).astype(F32)
    large = max_exact + (jnp.log(nf / max_exact) / math.log(MAX_DISTANCE / max_exact)
                         * (nb - max_exact)).astype(jnp.int32)
    large = jnp.minimum(large, nb - 1)
    return ret + jnp.where(is_small, n, large)


def _bias_kernel(tab_ref, lq1_ref, lk1_ref, lq2_ref, lk2_ref, bias_ref, lam_ref):
    h = pl.program_id(0)
    shape = (2 * TKV, 2 * TQ)
    c = lax.broadcasted_iota(jnp.int32, shape, 0)
    r = lax.broadcasted_iota(jnp.int32, shape, 1)
    r = jnp.where(r >= TQ, r - TQ, r)
    rel = c - TKV - r
    bucket = _t5_bucket(rel)
    far = tab_ref[h, NUM_BUCKETS // 2 - 1]
    val = jnp.zeros(shape, F32)
    for b in range(NUM_BUCKETS):
        val = jnp.where(bucket == b, (tab_ref[h, b] - far) * LOG2E, val)
    shift = CHUNK.bit_length() - 1
    allowed = lax.shift_right_logical(c, shift) <= lax.shift_right_logical(r, shift) + TKV // CHUNK
    bias_ref[0:TKV, :] = jnp.zeros((TKV, 2 * TQ), F32)
    bias_ref[TKV:3 * TKV, :] = jnp.where(allowed, val, MASK_VALUE)
    lam = (jnp.exp(jnp.sum(lq1_ref[...] * lk1_ref[...], axis=-1, keepdims=True))
           - jnp.exp(jnp.sum(lq2_ref[...] * lk2_ref[...], axis=-1, keepdims=True)) + LAM_INIT)
    lam_ref[...] = jnp.broadcast_to(lam, lam_ref.shape)


def _attn_bias(rel_bias, lq1, lk1, lq2, lk2):
    assert TQ == TKV and TQ > MAX_DISTANCE and TQ % CHUNK == 0
    tab = rel_bias.T.astype(F32)
    vec = pl.BlockSpec((1, HEAD_DIM), lambda h: (0, 0))
    return pl.pallas_call(
        _bias_kernel,
        grid=(N_HEADS,),
        in_specs=[pl.BlockSpec(memory_space=pltpu.SMEM), vec, vec, vec, vec],
        out_specs=[pl.BlockSpec((None, 3 * TKV, 2 * TQ), lambda h: (h, 0, 0)),
                   pl.BlockSpec((None, SUBLANE, LANE), lambda h: (h, 0, 0))],
        out_shape=[jax.ShapeDtypeStruct((N_HEADS, 3 * TKV, 2 * TQ), F32),
                   jax.ShapeDtypeStruct((N_HEADS, SUBLANE, LANE), F32)],
        name="attn_bias",
    )(tab, lq1.reshape(1, -1), lk1.reshape(1, -1), lq2.reshape(1, -1), lk2.reshape(1, -1))


def _inproj_kernel(x_ref, g_ref, b_ref, w_ref, wvt_ref, hn_ref, x0_ref, vt_ref, xb_scr):
    j = pl.program_id(1)

    @pl.when(j == 0)
    def _():
        y = _layer_norm(x_ref[...], g_ref[...], b_ref[...])
        x0_ref[...] = y
        yb = y.astype(BF16)
        xb_scr[...] = yb
        vt = lax.dot_general(wvt_ref[...], yb, (((1,), (1,)), ((), ())),
                             preferred_element_type=F32)
        row = lax.broadcasted_iota(jnp.int32, (V_AUG - V_DIM, TKV), 0)
        ones_row = jnp.where(row == 0, 1.0, 0.0).astype(BF16)
        for s in range(vt_ref.shape[0]):
            for h in range(N_HEADS):
                vt_ref[s, h, 0:V_DIM, :] = vt[h * V_DIM:(h + 1) * V_DIM,
                                              s * TKV:(s + 1) * TKV].astype(BF16)
                vt_ref[s, h, V_DIM:V_AUG, :] = ones_row

    hn_ref[...] = jnp.dot(xb_scr[...], w_ref[...], preferred_element_type=F32).astype(BF16)


def _in_proj(x2, ln_g, ln_b, w_nat, w_vt, batch, seq):
    t = x2.shape[0]
    tm, tn = TM_IN, TN_IN
    assert seq % tm == 0 and tm % TKV == 0 and NAT_COLS % tn == 0
    nblk = seq // tm
    const = lambda i, j: (0, 0)
    return pl.pallas_call(
        _inproj_kernel,
        grid=(t // tm, NAT_COLS // tn),
        in_specs=[pl.BlockSpec((tm, D_MODEL), lambda i, j: (i, 0)),
                  pl.BlockSpec((1, D_MODEL), const),
                  pl.BlockSpec((1, D_MODEL), const),
                  pl.BlockSpec((None, D_MODEL, tn), lambda i, j: (j, 0, 0)),
                  pl.BlockSpec((ATTN_WIDTH, D_MODEL), const)],
        out_specs=[pl.BlockSpec((tm, tn), lambda i, j: (i, j)),
                   pl.BlockSpec((tm, D_MODEL), lambda i, j: (i, 0)),
                   pl.BlockSpec((None, tm // TKV, N_HEADS, V_AUG, TKV),
                                lambda i, j: (i // nblk, i % nblk, 0, 0, 0))],
        out_shape=[jax.ShapeDtypeStruct((t, NAT_COLS), BF16),
                   jax.ShapeDtypeStruct((t, D_MODEL), F32),
                   jax.ShapeDtypeStruct((batch, seq // TKV, N_HEADS, V_AUG, TKV), BF16)],
        scratch_shapes=[pltpu.VMEM((tm, D_MODEL), BF16)],
        compiler_params=pltpu.CompilerParams(
            dimension_semantics=("arbitrary", "arbitrary"), vmem_limit_bytes=VMEM_LIMIT_BYTES),
        name="in_proj",
    )(x2, ln_g.reshape(1, -1), ln_b.reshape(1, -1), w_nat, w_vt)


def _attn_kernel(lam_ref, q_ref, k_ref, vt_ref, bias_ref, g_ref, o_ref,
                 q2_scr, s_scr, p_scr, al_scr, m_scr, acc_scr, *, nq):
    npairs = nq * (nq + 1) // 2
    lane = lax.broadcasted_iota(jnp.int32, (TQ, LANE), 1)

    def prep(i, carry):
        qs = q_ref[pl.ds(pl.multiple_of(i * TQ, TQ), TQ), :].astype(F32)
        q2_scr[i] = jnp.concatenate([jnp.where(lane < HEAD_DIM, qs, 0.0),
                                     jnp.where(lane >= HEAD_DIM, qs, 0.0)], axis=0).astype(BF16)
        m_scr[i] = jnp.full(m_scr.shape[1:], -jnp.inf, F32)
        acc_scr[i] = jnp.zeros(acc_scr.shape[1:], F32)
        return carry

    lax.fori_loop(0, nq, prep, 0)
    s_scr[1] = jnp.zeros(s_scr.shape[1:], F32)
    p_scr[0] = jnp.zeros(p_scr.shape[1:], BF16)
    al_scr[0] = jnp.ones(al_scr.shape[1:], F32)

    def advance(i, j):
        wrap = j >= i
        last = jnp.logical_and(i == nq - 1, wrap)
        ni = jnp.where(jnp.logical_and(wrap, jnp.logical_not(last)), i + 1, i)
        nj = jnp.where(last, j, jnp.where(wrap, 0, j + 1))
        return ni, nj

    def substep(t, sa, pairs):
        (ia, ja), (ib, jb), (ic, jc) = pairs
        sb = 1 - sa
        pv = jnp.dot(vt_ref[jc], p_scr[sa], preferred_element_type=F32)
        acc_scr[ic] = acc_scr[ic] * al_scr[sa] + pv
        live_b = jnp.logical_and(t >= 1, t <= npairs)
        st = s_scr[sb]
        m_prev = m_scr[ib]
        m_new = jnp.where(live_b, jnp.maximum(m_prev, jnp.max(st, axis=0, keepdims=True)), m_prev)
        al_scr[sb] = jnp.where(live_b, jnp.exp2(m_prev - m_new), 1.0)
        m_scr[ib] = m_new
        p_scr[sb] = jnp.exp2(st - jnp.where(live_b, m_new, jnp.inf)).astype(BF16)
        kblk = k_ref[pl.ds(pl.multiple_of(ja * TKV, TKV), TKV), :]
        boff = jnp.clip(ja - (ia - 2), 0, 2) * TKV
        s_scr[sa] = lax.dot_general(kblk, q2_scr[ia], (((1,), (1,)), ((), ())),
                                    preferred_element_type=F32) + bias_ref[pl.ds(pl.multiple_of(boff, TKV), TKV), :]
        return (advance(ia, ja), (ia, ja), (ib, jb))

    def body(tt, pairs):
        pairs = substep(2 * tt, 0, pairs)
        return substep(2 * tt + 1, 1, pairs)

    zero = jnp.int32(0)
    lax.fori_loop(0, (npairs + 3) // 2, body, ((zero, zero), (zero, zero), (zero, zero)))

    lam = lam_ref[0:1, 0:1]

    def finish(i, carry):
        acc = acc_scr[i]
        o = acc[0:V_DIM, :] / acc[V_DIM:V_DIM + 1, :]
        att = o[:, :TQ] - lam * o[:, TQ:]
        ms = jnp.mean(att * att, axis=0, keepdims=True)
        y = att * lax.rsqrt(ms + LN_EPS) * g_ref[...] * (1.0 - LAM_INIT)
        o_ref[pl.ds(pl.multiple_of(i * TQ, TQ), TQ), :] = y.T.astype(BF16)
        return carry

    lax.fori_loop(0, nq, finish, 0)


def _diff_attn(lam, hn, vt, bias, subln_g, batch, seq):
    t = hn.shape[0]
    nq = seq // TQ
    qb, kb = COL_Q // LANE, COL_K // LANE
    return pl.pallas_call(
        functools.partial(_attn_kernel, nq=nq),
        grid=(batch, N_HEADS),
        in_specs=[pl.BlockSpec((None, SUBLANE, LANE), lambda b, h: (h, 0, 0)),
                  pl.BlockSpec((seq, LANE), lambda b, h: (b, qb + h)),
                  pl.BlockSpec((seq, LANE), lambda b, h: (b, kb + h)),
                  pl.BlockSpec((None, seq // TKV, None, V_AUG, TKV), lambda b, h: (b, 0, h, 0, 0)),
                  pl.BlockSpec((None, 3 * TKV, 2 * TQ), lambda b, h: (h, 0, 0)),
                  pl.BlockSpec((V_DIM, 1), lambda b, h: (0, 0))],
        out_specs=pl.BlockSpec((seq, V_DIM), lambda b, h: (b, h)),
        out_shape=jax.ShapeDtypeStruct((t, ATTN_WIDTH), BF16),
        scratch_shapes=[pltpu.VMEM((nq, 2 * TQ, LANE), BF16),
                        pltpu.VMEM((2, TKV, 2 * TQ), F32), pltpu.VMEM((2, TKV, 2 * TQ), BF16),
                        pltpu.VMEM((2, 1, 2 * TQ), F32),
                        pltpu.VMEM((nq, 1, 2 * TQ), F32),
                        pltpu.VMEM((nq, V_AUG, 2 * TQ), F32)],
        compiler_params=pltpu.CompilerParams(
            dimension_semantics=("arbitrary", "arbitrary"),
            vmem_limit_bytes=VMEM_LIMIT_BYTES),
        name="diff_attn",
    )(lam, hn, hn, vt, bias, subln_g.reshape(-1, 1))


def _merge_kernel(gp_ref, xp_ref, halo_ref, ya_ref, x0_ref, p_ref, wgrp_ref, psc_ref, wbp_ref,
                  wba_ref, gb_ref, wout_ref, l1g_ref, l1b_ref, wg_ref, wp_ref,
                  x1_ref, r_ref, *, nblk):
    i = pl.program_id(0)
    tm = xp_ref.shape[0]
    blk_in_seq = i % nblk
    xp = xp_ref[...].astype(F32)
    halo = halo_ref[...].astype(F32)
    halo = jnp.where(blk_in_seq == 0, jnp.zeros_like(halo), halo)
    ext = jnp.concatenate([halo, xp], axis=0)
    pos = blk_in_seq * tm + lax.broadcasted_iota(jnp.int32, (tm, 1), 0)

    yps = []
    for gi, w in enumerate(POOL_WINDOWS):
        lo, hi = gi * POOL_GROUP_DIM, (gi + 1) * POOL_GROUP_DIM
        s = ext[:, lo:hi]
        span = 1
        while span < w:
            s = s[span:] + s[:-span]
            span *= 2
        start = HALO - (w - 1)
        win = s[start:start + tm]
        cnt = jnp.minimum(pos + 1, w).astype(F32)
        mixed = win / cnt - xp[:, lo:hi]
        yps.append(jnp.dot(mixed.astype(BF16), wgrp_ref[gi], preferred_element_type=F32))
    yp = jnp.concatenate(yps, axis=1) * psc_ref[...]

    gates = _sigmoid(gp_ref[...].astype(F32) + gb_ref[...])
    bp = jnp.dot(yp.astype(BF16), wbp_ref[...], preferred_element_type=F32)
    ba = jnp.dot(ya_ref[...], wba_ref[...], preferred_element_type=F32)
    merged = gates[:, :D_MODEL] * bp + gates[:, D_MODEL:] * ba
    y = jnp.dot(merged.astype(BF16), wout_ref[...], preferred_element_type=F32)
    x1 = _layer_norm(ALPHA * x0_ref[...] + y, l1g_ref[...], l1b_ref[...])
    x1_ref[...] = x1

    x1b = x1.astype(BF16)
    e = (_sigmoid(jnp.dot(x1b, wg_ref[...], preferred_element_type=F32))
         * jnp.dot(p_ref[...].astype(BF16), wp_ref[...], preferred_element_type=F32))
    r_ref[...] = ALPHA * x1 + e


def _merge(hn, ya, x0, p2, wgrp, psc, wbp, wba, gate_b, wout, l1g, l1b, wg, wp, seq):
    t = hn.shape[0]
    tm = TM_MERGE
    assert seq % tm == 0 and tm % HALO == 0
    nblk = seq // tm
    const2 = lambda i: (0, 0)
    single = pl.Buffered(1)
    wspec = lambda shape: pl.BlockSpec(shape, (lambda i: (0,) * len(shape)), pipeline_mode=single)
    return pl.pallas_call(
        functools.partial(_merge_kernel, nblk=nblk),
        grid=(t // tm,),
        in_specs=[pl.BlockSpec((tm, GATE_WIDTH), lambda i: (i, COL_G // GATE_WIDTH)),
                  pl.BlockSpec((tm, POOL_WIDTH), lambda i: (i, COL_P // POOL_WIDTH)),
                  pl.BlockSpec((HALO, POOL_WIDTH),
                               lambda i: (jnp.maximum(i * (tm // HALO) - 1, 0), COL_P // POOL_WIDTH)),
                  pl.BlockSpec((tm, ATTN_WIDTH), lambda i: (i, 0)),
                  pl.BlockSpec((tm, D_MODEL), lambda i: (i, 0)),
                  pl.BlockSpec((tm, PLE_DIM), lambda i: (i, 0)),
                  wspec((POOL_GROUPS, POOL_GROUP_DIM, POOL_GROUP_DIM)),
                  wspec((1, POOL_WIDTH)),
                  wspec((POOL_WIDTH, D_MODEL)),
                  wspec((ATTN_WIDTH, D_MODEL)),
                  wspec((1, GATE_WIDTH)),
                  wspec((D_MODEL, D_MODEL)),
                  wspec((1, D_MODEL)),
                  wspec((1, D_MODEL)),
                  wspec((D_MODEL, D_MODEL)),
                  wspec((PLE_DIM, D_MODEL))],
        out_specs=[pl.BlockSpec((tm, D_MODEL), lambda i: (i, 0)),
                   pl.BlockSpec((tm, D_MODEL), lambda i: (i, 0))],
        out_shape=[jax.ShapeDtypeStruct((t, D_MODEL), F32),
                   jax.ShapeDtypeStruct((t, D_MODEL), F32)],
        compiler_params=pltpu.CompilerParams(
            dimension_semantics=("arbitrary",), vmem_limit_bytes=VMEM_LIMIT_BYTES),
        name="merge",
    )(hn, hn, hn, ya, x0, p2, wgrp, psc, wbp, wba, gate_b, wout, l1g, l1b, wg, wp)


_NTOP = PEER_TOPK + 1
_CAND = [(a, b) for a in range(_NTOP) for b in range(_NTOP)
         if (a + 1) * (b + 1) <= _NTOP]
_CAND_ROWS = -(-len(_CAND) // SUBLANE) * SUBLANE


def _top_values(s, k):
    vals = []
    for _ in range(k):
        m = jnp.max(s, axis=0, keepdims=True)
        vals.append(m)
        s = jnp.where(s == m, -jnp.inf, s)
    return vals


def _sorting_network(n):
    pairs = []
    p = 1
    while p < n:
        k = p
        while k >= 1:
            for j in range(k % p, n - k, 2 * k):
                for i in range(min(k, n - j - k)):
                    if (i + j) // (2 * p) == (i + j + k) // (2 * p):
                        pairs.append((i + j, i + j + k))
            k //= 2
        p *= 2
    return pairs


_SORT16 = _sorting_network(N_KEYS // SUBLANE)


def _top_values_keys(s, k):
    ng = N_KEYS // SUBLANE
    v = [s[g * SUBLANE:(g + 1) * SUBLANE, :] for g in range(ng)]
    for i, j in _SORT16:
        v[i], v[j] = jnp.maximum(v[i], v[j]), jnp.minimum(v[i], v[j])
    vals = []
    for r in range(k):
        m = jnp.max(v[0], axis=0, keepdims=True)
        vals.append(m)
        left = k - 1 - r
        if left == 0:
            break
        hit = v[0] == m
        for d in range(min(left, ng - 1)):
            v[d] = jnp.where(hit, v[d + 1], v[d])
        if left >= ng:
            v[ng - 1] = jnp.where(hit, -jnp.inf, v[ng - 1])
    return vals


def _peer_kernel(x1_ref, r_ref, wqt_ref, k1_ref, k2_ref, u_ref, vt_ref, g_ref, b_ref, o_ref,
                 xt_scr, thr_scr, s2_scr, kap_scr, cand_scr, h_scr, a_scr, acc_scr, *, nc):
    c = pl.program_id(1)
    tm = x1_ref.shape[0]
    ec = u_ref.shape[0]

    @pl.when(c == 0)
    def _():
        xt = x1_ref[...].T.astype(BF16)
        xt_scr[...] = xt
        cand_scr[...] = jnp.full(cand_scr.shape, -jnp.inf, F32)
        qt = jnp.dot(wqt_ref[...], xt, preferred_element_type=F32)
        for h in range(PEER_HEADS):
            base = h * 2 * PEER_HALF
            s1 = jnp.dot(k1_ref[...], qt[base:base + PEER_HALF].astype(BF16),
                         preferred_element_type=F32)
            s2 = jnp.dot(k2_ref[...], qt[base + PEER_HALF:base + 2 * PEER_HALF].astype(BF16),
                         preferred_element_type=F32)
            v1 = _top_values_keys(s1, _NTOP)
            v2 = _top_values_keys(s2, _NTOP)
            for n, (a, b) in enumerate(_CAND):
                cand_scr[n:n + 1, :] = v1[a] + v2[b]
            cand = cand_scr[...]
            top = _top_values(cand, _NTOP)
            tau = 0.5 * (top[PEER_TOPK - 1] + top[PEER_TOPK])
            cmax = top[0]
            z = jnp.sum(jnp.where(cand >= tau, jnp.exp(cand - cmax), 0.0), axis=0, keepdims=True)
            logz = cmax + jnp.log(z)
            thr = (logz - s1) * LOG2E + 1.0
            for lt in range(tm // LANE):
                thr_scr[h, lt] = thr[:, lt * LANE:(lt + 1) * LANE]
            s2_scr[h] = s2 * LOG2E
            kap_scr[h] = (tau - logz) * LOG2E - 1.0
        acc_scr[...] = jnp.zeros(acc_scr.shape, F32)

    slot = c % 2
    nal = ec // N_KEYS
    kb_rows = N_KEYS // 2

    def stage_a():
        h_scr[slot] = jnp.dot(u_ref[...], xt_scr[...], preferred_element_type=F32)

    def stage_b():
        for lt in range(tm // LANE):
            cols = slice(lt * LANE, (lt + 1) * LANE)
            for half in range(N_KEYS // kb_rows):
                rows = slice(half * kb_rows, (half + 1) * kb_rows)
                w = [jnp.zeros((kb_rows, LANE), F32) for _ in range(nal)]
                for h in range(PEER_HEADS):
                    s2t = s2_scr[h, rows, cols]
                    kap = kap_scr[h, :, cols]
                    for al in range(nal):
                        thr = thr_scr[h, lt, pl.ds((c - 1) * nal + al, 1), :]
                        x = s2t - thr
                        w[al] = w[al] + jnp.where(x >= kap, jnp.exp2(x), 0.0)
                for al in range(nal):
                    r0 = al * N_KEYS + half * kb_rows
                    hv = h_scr[1 - slot, r0:r0 + kb_rows, cols]
                    a = hv * (1.0 + lax.erf(hv * np.float32(math.sqrt(0.5)))) * w[al]
                    a_scr[1 - slot, r0:r0 + kb_rows, cols] = a.astype(BF16)

    def stage_c():
        acc_scr[...] += jnp.dot(vt_ref[...], a_scr[slot], preferred_element_type=F32)

    @pl.when(jnp.logical_and(c >= 2, c < nc))
    def _():
        stage_c()
        stage_b()
        stage_a()

    @pl.when(c == 0)
    def _():
        stage_a()

    @pl.when(c == 1)
    def _():
        stage_b()
        stage_a()

    @pl.when(c == nc)
    def _():
        stage_c()
        stage_b()

    @pl.when(c == nc + 1)
    def _():
        stage_c()


    @pl.when(c == pl.num_programs(1) - 1)
    def _():
        y = r_ref[...] + acc_scr[...].T
        o_ref[...] = _layer_norm(y, g_ref[...], b_ref[...])


def _peer(x1, r, wqt, k1, k2, u, vt, l2g, l2b):
    t = x1.shape[0]
    tm, ec = TM_PEER, EC_PEER
    assert t % tm == 0 and N_EXPERTS % ec == 0 and ec % N_KEYS == 0
    nc = N_EXPERTS // ec
    single = pl.Buffered(1)
    row = lambda i, c: (i, 0)
    const = lambda i, c: (0, 0)
    stat = pltpu.VMEM((PEER_HEADS, N_KEYS, tm), F32)
    return pl.pallas_call(
        functools.partial(_peer_kernel, nc=nc),
        grid=(t // tm, nc + 2),
        in_specs=[pl.BlockSpec((tm, D_MODEL), row, pipeline_mode=single),
                  pl.BlockSpec((tm, D_MODEL), row, pipeline_mode=single),
                  pl.BlockSpec((PEER_HEADS * 2 * PEER_HALF, D_MODEL), const, pipeline_mode=single),
                  pl.BlockSpec((N_KEYS, PEER_HALF), const, pipeline_mode=single),
                  pl.BlockSpec((N_KEYS, PEER_HALF), const, pipeline_mode=single),
                  pl.BlockSpec((ec, D_MODEL), lambda i, c: (jnp.minimum(c, nc - 1), 0)),
                  pl.BlockSpec((None, D_MODEL, ec), lambda i, c: (jnp.clip(c - 2, 0, nc - 1), 0, 0)),
                  pl.BlockSpec((1, D_MODEL), const, pipeline_mode=single),
                  pl.BlockSpec((1, D_MODEL), const, pipeline_mode=single)],
        out_specs=pl.BlockSpec((tm, D_MODEL), row, pipeline_mode=single),
        out_shape=jax.ShapeDtypeStruct((t, D_MODEL), F32),
        scratch_shapes=[pltpu.VMEM((D_MODEL, tm), BF16),
                        pltpu.VMEM((PEER_HEADS, tm // LANE, N_KEYS, LANE), F32), stat,
                        pltpu.VMEM((PEER_HEADS, 1, tm), F32),
                        pltpu.VMEM((_CAND_ROWS, tm), F32),
                        pltpu.VMEM((2, ec, tm), F32), pltpu.VMEM((2, ec, tm), BF16),
                        pltpu.VMEM((D_MODEL, tm), F32)],
        compiler_params=pltpu.CompilerParams(
            dimension_semantics=("arbitrary", "arbitrary"), vmem_limit_bytes=VMEM_LIMIT_BYTES,
        ),
        name="peer",
    )(x1, r, wqt, k1, k2, u, vt, l2g.reshape(1, -1), l2b.reshape(1, -1))


def kernel(x, p, ln_in_g, ln_in_b, w_in, pool_grp_w, pool_scale, lambda_q1, lambda_k1,
           lambda_q2, lambda_k2, subln_g, w_branch_pool, w_branch_attn, gate_b, w_out,
           ln1_g, ln1_b, peer_wq, peer_keys1, peer_keys2, peer_u, peer_v, ple_w, ple_gate_w,
           ln2_g, ln2_b, rel_bias):
    batch, seq, d = x.shape
    assert d == D_MODEL and w_in.shape[0] == DEPTH == 1
    t = batch * seq
    c0, c1 = POOL_WIDTH, POOL_WIDTH + QK_WIDTH
    c2 = c1 + QK_WIDTH
    c3 = c2 + ATTN_WIDTH

    w = w_in[0]
    wq = w[:, c0:c1] * (HEAD_DIM ** -0.5 * LOG2E)
    w_nat = jnp.concatenate([w[:, c3:], w[:, :c0], wq, w[:, c1:c2]], axis=1).astype(BF16)
    w_nat = w_nat.reshape(D_MODEL, NAT_COLS // TN_IN, TN_IN).transpose(1, 0, 2)
    w_vt = w[:, c2:c3].T.astype(BF16)

    bias, lam = _attn_bias(rel_bias, lambda_q1[0], lambda_k1[0], lambda_q2[0], lambda_k2[0])
    hn, x0, vt = _in_proj(x.reshape(t, d), ln_in_g, ln_in_b, w_nat, w_vt, batch, seq)
    ya = _diff_attn(lam, hn, vt, bias, subln_g[0], batch, seq)
    x1, r = _merge(hn, ya, x0, p[0].reshape(t, PLE_DIM),
                   pool_grp_w[0].astype(BF16), pool_scale[0].reshape(1, -1),
                   w_branch_pool[0].astype(BF16), w_branch_attn[0].astype(BF16),
                   gate_b[0].reshape(1, -1), w_out[0].astype(BF16),
                   ln1_g[0].reshape(1, -1), ln1_b[0].reshape(1, -1),
                   ple_gate_w[0].astype(BF16), ple_w[0].astype(BF16), seq)
    out = _peer(x1, r, peer_wq[0].T.astype(BF16), peer_keys1[0].astype(BF16),
                peer_keys2[0].astype(BF16), peer_u[0].astype(BF16),
                peer_v[0].astype(BF16).reshape(N_EXPERTS // EC_PEER, EC_PEER, d).transpose(0, 2, 1),
                ln2_g[0], ln2_b[0])
    return out.reshape(batch, seq, d)
```

```python
import functools
import math

import jax
import jax.numpy as jnp
import numpy as np
from jax import lax
from jax.experimental import pallas as pl
from jax.experimental.pallas import tpu as pltpu

F32 = jnp.float32
BF16 = jnp.bfloat16

D_MODEL = 2048
DEPTH = 1
CHUNK = 64
POOL_WINDOWS = (2, 4, 8, 16)
POOL_GROUPS = 4
POOL_WIDTH = D_MODEL // 2
POOL_GROUP_DIM = POOL_WIDTH // POOL_GROUPS
N_HEADS = 8
HEAD_DIM = D_MODEL // 32
V_DIM = 2 * HEAD_DIM
QK_WIDTH = N_HEADS * 2 * HEAD_DIM
ATTN_WIDTH = N_HEADS * V_DIM
N_BRANCHES = 2
GATE_WIDTH = N_BRANCHES * D_MODEL
NUM_BUCKETS = 32
MAX_DISTANCE = 128
PEER_HEADS = 8
N_KEYS = 128
N_EXPERTS = N_KEYS * N_KEYS
PEER_HALF = 128
PEER_TOPK = 16
PLE_DIM = 256
ALPHA = (2 * DEPTH) ** 0.25
LN_EPS = 1e-5
MASK_VALUE = -1e30
LAM_INIT = 0.8 - 0.6 * math.exp(-0.3 * 0)

LANE = 128
SUBLANE = 8
VMEM_LIMIT_BYTES = 60 * 1024 * 1024

V_AUG = V_DIM + 16
LOG2E = math.log2(math.e)
TQ = 256
TKV = 256
ATTN_UNROLL = 4
HALO = 16
TM_IN = 512
TN_IN = 1024
TM_MERGE = 256
TM_PEER = 512
EC_PEER = 512

COL_G = 0
COL_P = GATE_WIDTH
COL_Q = COL_P + POOL_WIDTH
COL_K = COL_Q + QK_WIDTH
NAT_COLS = COL_K + QK_WIDTH


def _layer_norm(x, g, b):
    mu = jnp.mean(x, axis=-1, keepdims=True)
    xc = x - mu
    var = jnp.mean(xc * xc, axis=-1, keepdims=True)
    return xc * lax.rsqrt(var + LN_EPS) * g + b


def _sigmoid(x):
    return 1.0 / (1.0 + jnp.exp(-x))


def _t5_bucket(rel):
    nb = NUM_BUCKETS // 2
    ret = (rel > 0).astype(jnp.int32) * nb
    n = jnp.abs(rel)
    max_exact = nb // 2
    is_small = n < max_exact
    nf = jnp.maximum(n, 1).astype(F32)
    large = max_exact + (jnp.log(nf / max_exact) / math.log(MAX_DISTANCE / max_exact)
                         * (nb - max_exact)).astype(jnp.int32)
    large = jnp.minimum(large, nb - 1)
    return ret + jnp.where(is_small, n, large)


def _bias_kernel(tab_ref, lq1_ref, lk1_ref, lq2_ref, lk2_ref, bias_ref, lam_ref):
    h = pl.program_id(0)
    shape = (2 * TKV, 2 * TQ)
    c = lax.broadcasted_iota(jnp.int32, shape, 0)
    r = lax.broadcasted_iota(jnp.int32, shape, 1)
    r = jnp.where(r >= TQ, r - TQ, r)
    rel = c - TKV - r
    bucket = _t5_bucket(rel)
    far = tab_ref[h, NUM_BUCKETS // 2 - 1]
    val = jnp.zeros(shape, F32)
    for b in range(NUM_BUCKETS):
        val = jnp.where(bucket == b, (tab_ref[h, b] - far) * LOG2E, val)
    shift = CHUNK.bit_length() - 1
    allowed = lax.shift_right_logical(c, shift) <= lax.shift_right_logical(r, shift) + TKV // CHUNK
    bias_ref[0:TKV, :] = jnp.zeros((TKV, 2 * TQ), F32)
    bias_ref[TKV:3 * TKV, :] = jnp.where(allowed, val, MASK_VALUE)
    lam = (jnp.exp(jnp.sum(lq1_ref[...] * lk1_ref[...], axis=-1, keepdims=True))
           - jnp.exp(jnp.sum(lq2_ref[...] * lk2_ref[...], axis=-1, keepdims=True)) + LAM_INIT)
    lam_ref[...] = jnp.broadcast_to(lam, lam_ref.shape)


def _attn_bias(rel_bias, lq1, lk1, lq2, lk2):
    assert TQ == TKV and TQ > MAX_DISTANCE and TQ % CHUNK == 0
    tab = rel_bias.T.astype(F32)
    vec = pl.BlockSpec((1, HEAD_DIM), lambda h: (0, 0))
    return pl.pallas_call(
        _bias_kernel,
        grid=(N_HEADS,),
        in_specs=[pl.BlockSpec(memory_space=pltpu.SMEM), vec, vec, vec, vec],
        out_specs=[pl.BlockSpec((None, 3 * TKV, 2 * TQ), lambda h: (h, 0, 0)),
                   pl.BlockSpec((None, SUBLANE, LANE), lambda h: (h, 0, 0))],
        out_shape=[jax.ShapeDtypeStruct((N_HEADS, 3 * TKV, 2 * TQ), F32),
                   jax.ShapeDtypeStruct((N_HEADS, SUBLANE, LANE), F32)],
        name="attn_bias",
    )(tab, lq1.reshape(1, -1), lk1.reshape(1, -1), lq2.reshape(1, -1), lk2.reshape(1, -1))


def _inproj_kernel(x_ref, g_ref, b_ref, w_ref, wvt_ref, hn_ref, x0_ref, vt_ref, xb_scr):
    j = pl.program_id(1)

    @pl.when(j == 0)
    def _():
        y = _layer_norm(x_ref[...], g_ref[...], b_ref[...])
        x0_ref[...] = y
        yb = y.astype(BF16)
        xb_scr[...] = yb
        vt = lax.dot_general(wvt_ref[...], yb, (((1,), (1,)), ((), ())),
                             preferred_element_type=F32)
        row = lax.broadcasted_iota(jnp.int32, (V_AUG - V_DIM, TKV), 0)
        ones_row = jnp.where(row == 0, 1.0, 0.0).astype(BF16)
        for s in range(vt_ref.shape[0]):
            for h in range(N_HEADS):
                vt_ref[s, h, 0:V_DIM, :] = vt[h * V_DIM:(h + 1) * V_DIM,
                                              s * TKV:(s + 1) * TKV].astype(BF16)
                vt_ref[s, h, V_DIM:V_AUG, :] = ones_row

    hn_ref[...] = jnp.dot(xb_scr[...], w_ref[...], preferred_element_type=F32).astype(BF16)


def _in_proj(x2, ln_g, ln_b, w_nat, w_vt, batch, seq):
    t = x2.shape[0]
    tm, tn = TM_IN, TN_IN
    assert seq % tm == 0 and tm % TKV == 0 and NAT_COLS % tn == 0
    nblk = seq // tm
    const = lambda i, j: (0, 0)
    return pl.pallas_call(
        _inproj_kernel,
        grid=(t // tm, NAT_COLS // tn),
        in_specs=[pl.BlockSpec((tm, D_MODEL), lambda i, j: (i, 0)),
                  pl.BlockSpec((1, D_MODEL), const),
                  pl.BlockSpec((1, D_MODEL), const),
                  pl.BlockSpec((None, D_MODEL, tn), lambda i, j: (j, 0, 0)),
                  pl.BlockSpec((ATTN_WIDTH, D_MODEL), const)],
        out_specs=[pl.BlockSpec((tm, tn), lambda i, j: (i, j)),
                   pl.BlockSpec((tm, D_MODEL), lambda i, j: (i, 0)),
                   pl.BlockSpec((None, tm // TKV, N_HEADS, V_AUG, TKV),
                                lambda i, j: (i // nblk, i % nblk, 0, 0, 0))],
        out_shape=[jax.ShapeDtypeStruct((t, NAT_COLS), BF16),
                   jax.ShapeDtypeStruct((t, D_MODEL), F32),
                   jax.ShapeDtypeStruct((batch, seq // TKV, N_HEADS, V_AUG, TKV), BF16)],
        scratch_shapes=[pltpu.VMEM((tm, D_MODEL), BF16)],
        compiler_params=pltpu.CompilerParams(
            dimension_semantics=("arbitrary", "arbitrary"), vmem_limit_bytes=VMEM_LIMIT_BYTES),
        name="in_proj",
    )(x2, ln_g.reshape(1, -1), ln_b.reshape(1, -1), w_nat, w_vt)


def _attn_kernel(lam_ref, q_ref, k_ref, vt_ref, bias_ref, g_ref, o_ref,
                 q2_scr, s_scr, p_scr, al_scr, m_scr, acc_scr, *, nq):
    npairs = nq * (nq + 1) // 2
    lane = lax.broadcasted_iota(jnp.int32, (TQ, LANE), 1)

    def prep(i, carry):
        qs = q_ref[pl.ds(pl.multiple_of(i * TQ, TQ), TQ), :].astype(F32)
        q2_scr[i] = jnp.concatenate([jnp.where(lane < HEAD_DIM, qs, 0.0),
                                     jnp.where(lane >= HEAD_DIM, qs, 0.0)], axis=0).astype(BF16)
        m_scr[i] = jnp.full(m_scr.shape[1:], -jnp.inf, F32)
        acc_scr[i] = jnp.zeros(acc_scr.shape[1:], F32)
        return carry

    lax.fori_loop(0, nq, prep, 0)
    s_scr[1] = jnp.zeros(s_scr.shape[1:], F32)
    p_scr[0] = jnp.zeros(p_scr.shape[1:], BF16)
    al_scr[0] = jnp.ones(al_scr.shape[1:], F32)

    def advance(i, j):
        wrap = j >= i
        last = jnp.logical_and(i == nq - 1, wrap)
        ni = jnp.where(jnp.logical_and(wrap, jnp.logical_not(last)), i + 1, i)
        nj = jnp.where(last, j, jnp.where(wrap, 0, j + 1))
        return ni, nj

    def substep(t, sa, pairs):
        (ia, ja), (ib, jb), (ic, jc) = pairs
        sb = 1 - sa
        pv = jnp.dot(vt_ref[jc], p_scr[sa], preferred_element_type=F32)
        acc_scr[ic] = acc_scr[ic] * al_scr[sa] + pv
        live_b = jnp.logical_and(t >= 1, t <= npairs)
        st = s_scr[sb]
        m_prev = m_scr[ib]
        m_new = jnp.where(live_b, jnp.maximum(m_prev, jnp.max(st, axis=0, keepdims=True)), m_prev)
        al_scr[sb] = jnp.where(live_b, jnp.exp2(m_prev - m_new), 1.0)
        m_scr[ib] = m_new
        p_scr[sb] = jnp.exp2(st - jnp.where(live_b, m_new, jnp.inf)).astype(BF16)
        kblk = k_ref[pl.ds(pl.multiple_of(ja * TKV, TKV), TKV), :]
        boff = jnp.clip(ja - (ia - 2), 0, 2) * TKV
        s_scr[sa] = lax.dot_general(kblk, q2_scr[ia], (((1,), (1,)), ((), ())),
                                    preferred_element_type=F32) + bias_ref[pl.ds(pl.multiple_of(boff, TKV), TKV), :]
        return (advance(ia, ja), (ia, ja), (ib, jb))

    def body(tt, pairs):
        for u in range(ATTN_UNROLL):
            pairs = substep(ATTN_UNROLL * tt + u, u % 2, pairs)
        return pairs

    zero = jnp.int32(0)
    lax.fori_loop(0, pl.cdiv(npairs + 2, ATTN_UNROLL), body, ((zero, zero), (zero, zero), (zero, zero)))

    lam = lam_ref[0:1, 0:1]

    def finish(i, carry):
        acc = acc_scr[i]
        o = acc[0:V_DIM, :] / acc[V_DIM:V_DIM + 1, :]
        att = o[:, :TQ] - lam * o[:, TQ:]
        ms = jnp.mean(att * att, axis=0, keepdims=True)
        y = att * lax.rsqrt(ms + LN_EPS) * g_ref[...] * (1.0 - LAM_INIT)
        o_ref[pl.ds(pl.multiple_of(i * TQ, TQ), TQ), :] = y.T.astype(BF16)
        return carry

    lax.fori_loop(0, nq, finish, 0)


def _diff_attn(lam, hn, vt, bias, subln_g, batch, seq):
    t = hn.shape[0]
    nq = seq // TQ
    qb, kb = COL_Q // LANE, COL_K // LANE
    return pl.pallas_call(
        functools.partial(_attn_kernel, nq=nq),
        grid=(batch, N_HEADS),
        in_specs=[pl.BlockSpec((None, SUBLANE, LANE), lambda b, h: (h, 0, 0)),
                  pl.BlockSpec((seq, LANE), lambda b, h: (b, qb + h)),
                  pl.BlockSpec((seq, LANE), lambda b, h: (b, kb + h)),
                  pl.BlockSpec((None, seq // TKV, None, V_AUG, TKV), lambda b, h: (b, 0, h, 0, 0)),
                  pl.BlockSpec((None, 3 * TKV, 2 * TQ), lambda b, h: (h, 0, 0)),
                  pl.BlockSpec((V_DIM, 1), lambda b, h: (0, 0))],
        out_specs=pl.BlockSpec((seq, V_DIM), lambda b, h: (b, h)),
        out_shape=jax.ShapeDtypeStruct((t, ATTN_WIDTH), BF16),
        scratch_shapes=[pltpu.VMEM((nq, 2 * TQ, LANE), BF16),
                        pltpu.VMEM((2, TKV, 2 * TQ), F32), pltpu.VMEM((2, TKV, 2 * TQ), BF16),
                        pltpu.VMEM((2, 1, 2 * TQ), F32),
                        pltpu.VMEM((nq, 1, 2 * TQ), F32),
                        pltpu.VMEM((nq, V_AUG, 2 * TQ), F32)],
        compiler_params=pltpu.CompilerParams(
            dimension_semantics=("arbitrary", "arbitrary"),
            vmem_limit_bytes=VMEM_LIMIT_BYTES),
        name="diff_attn",
    )(lam, hn, hn, vt, bias, subln_g.reshape(-1, 1))


def _merge_kernel(gp_ref, xp_ref, halo_ref, ya_ref, x0_ref, p_ref, wgrp_ref, psc_ref, wbp_ref,
                  wba_ref, gb_ref, wout_ref, l1g_ref, l1b_ref, wg_ref, wp_ref,
                  x1_ref, r_ref, *, nblk):
    i = pl.program_id(0)
    tm = xp_ref.shape[0]
    blk_in_seq = i % nblk
    xp = xp_ref[...].astype(F32)
    halo = halo_ref[...].astype(F32)
    halo = jnp.where(blk_in_seq == 0, jnp.zeros_like(halo), halo)
    ext = jnp.concatenate([halo, xp], axis=0)
    pos = blk_in_seq * tm + lax.broadcasted_iota(jnp.int32, (tm, 1), 0)

    yps = []
    for gi, w in enumerate(POOL_WINDOWS):
        lo, hi = gi * POOL_GROUP_DIM, (gi + 1) * POOL_GROUP_DIM
        s = ext[:, lo:hi]
        span = 1
        while span < w:
            s = s[span:] + s[:-span]
            span *= 2
        start = HALO - (w - 1)
        win = s[start:start + tm]
        cnt = jnp.minimum(pos + 1, w).astype(F32)
        mixed = win / cnt - xp[:, lo:hi]
        yps.append(jnp.dot(mixed.astype(BF16), wgrp_ref[gi], preferred_element_type=F32))
    yp = jnp.concatenate(yps, axis=1) * psc_ref[...]

    gates = _sigmoid(gp_ref[...].astype(F32) + gb_ref[...])
    bp = jnp.dot(yp.astype(BF16), wbp_ref[...], preferred_element_type=F32)
    ba = jnp.dot(ya_ref[...], wba_ref[...], preferred_element_type=F32)
    merged = gates[:, :D_MODEL] * bp + gates[:, D_MODEL:] * ba
    y = jnp.dot(merged.astype(BF16), wout_ref[...], preferred_element_type=F32)
    x1 = _layer_norm(ALPHA * x0_ref[...] + y, l1g_ref[...], l1b_ref[...])
    x1_ref[...] = x1

    x1b = x1.astype(BF16)
    e = (_sigmoid(jnp.dot(x1b, wg_ref[...], preferred_element_type=F32))
         * jnp.dot(p_ref[...].astype(BF16), wp_ref[...], preferred_element_type=F32))
    r_ref[...] = ALPHA * x1 + e


def _merge(hn, ya, x0, p2, wgrp, psc, wbp, wba, gate_b, wout, l1g, l1b, wg, wp, seq):
    t = hn.shape[0]
    tm = TM_MERGE
    assert seq % tm == 0 and tm % HALO == 0
    nblk = seq // tm
    const2 = lambda i: (0, 0)
    single = pl.Buffered(1)
    wspec = lambda shape: pl.BlockSpec(shape, (lambda i: (0,) * len(shape)), pipeline_mode=single)
    return pl.pallas_call(
        functools.partial(_merge_kernel, nblk=nblk),
        grid=(t // tm,),
        in_specs=[pl.BlockSpec((tm, GATE_WIDTH), lambda i: (i, COL_G // GATE_WIDTH)),
                  pl.BlockSpec((tm, POOL_WIDTH), lambda i: (i, COL_P // POOL_WIDTH)),
                  pl.BlockSpec((HALO, POOL_WIDTH),
                               lambda i: (jnp.maximum(i * (tm // HALO) - 1, 0), COL_P // POOL_WIDTH)),
                  pl.BlockSpec((tm, ATTN_WIDTH), lambda i: (i, 0)),
                  pl.BlockSpec((tm, D_MODEL), lambda i: (i, 0)),
                  pl.BlockSpec((tm, PLE_DIM), lambda i: (i, 0)),
                  wspec((POOL_GROUPS, POOL_GROUP_DIM, POOL_GROUP_DIM)),
                  wspec((1, POOL_WIDTH)),
                  wspec((POOL_WIDTH, D_MODEL)),
                  wspec((ATTN_WIDTH, D_MODEL)),
                  wspec((1, GATE_WIDTH)),
                  wspec((D_MODEL, D_MODEL)),
                  wspec((1, D_MODEL)),
                  wspec((1, D_MODEL)),
                  wspec((D_MODEL, D_MODEL)),
                  wspec((PLE_DIM, D_MODEL))],
        out_specs=[pl.BlockSpec((tm, D_MODEL), lambda i: (i, 0)),
                   pl.BlockSpec((tm, D_MODEL), lambda i: (i, 0))],
        out_shape=[jax.ShapeDtypeStruct((t, D_MODEL), F32),
                   jax.ShapeDtypeStruct((t, D_MODEL), F32)],
        compiler_params=pltpu.CompilerParams(
            dimension_semantics=("arbitrary",), vmem_limit_bytes=VMEM_LIMIT_BYTES),
        name="merge",
    )(hn, hn, hn, ya, x0, p2, wgrp, psc, wbp, wba, gate_b, wout, l1g, l1b, wg, wp)


_NTOP = PEER_TOPK + 1
_CAND = [(a, b) for a in range(_NTOP) for b in range(_NTOP)
         if (a + 1) * (b + 1) <= _NTOP]
_CAND_ROWS = -(-len(_CAND) // SUBLANE) * SUBLANE


def _sorting_network(n):
    pairs = []
    p = 1
    while p < n:
        k = p
        while k >= 1:
            for j in range(k % p, n - k, 2 * k):
                for i in range(min(k, n - j - k)):
                    if (i + j) // (2 * p) == (i + j + k) // (2 * p):
                        pairs.append((i + j, i + j + k))
            k //= 2
        p *= 2
    return pairs


def _top_values_sorted(s, k):
    ng = s.shape[0] // SUBLANE
    wires = 1 << (ng - 1).bit_length()
    v = [s[g * SUBLANE:(g + 1) * SUBLANE, :] for g in range(ng)]
    v += [jnp.full_like(v[0], -jnp.inf)] * (wires - ng)
    for i, j in _sorting_network(wires):
        v[i], v[j] = jnp.maximum(v[i], v[j]), jnp.minimum(v[i], v[j])
    v = v[:ng]
    vals = []
    for r in range(k):
        m = jnp.max(v[0], axis=0, keepdims=True)
        vals.append(m)
        left = k - 1 - r
        if left == 0:
            break
        hit = v[0] == m
        for d in range(min(left, ng - 1)):
            v[d] = jnp.where(hit, v[d + 1], v[d])
        if left >= ng:
            v[ng - 1] = jnp.where(hit, -jnp.inf, v[ng - 1])
    return vals


def _peer_kernel(x1_ref, r_ref, wqt_ref, k1_ref, k2_ref, u_ref, vt_ref, g_ref, b_ref, o_ref,
                 xt_scr, thr_scr, s2_scr, kap_scr, cand_scr, h_scr, a_scr, acc_scr, *, nc):
    c = pl.program_id(1)
    tm = x1_ref.shape[0]
    ec = u_ref.shape[0]

    @pl.when(c == 0)
    def _():
        xt = x1_ref[...].T.astype(BF16)
        xt_scr[...] = xt
        cand_scr[...] = jnp.full(cand_scr.shape, -jnp.inf, F32)
        qt = jnp.dot(wqt_ref[...], xt, preferred_element_type=F32)
        for h in range(PEER_HEADS):
            base = h * 2 * PEER_HALF
            s1 = jnp.dot(k1_ref[...], qt[base:base + PEER_HALF].astype(BF16),
                         preferred_element_type=F32)
            s2 = jnp.dot(k2_ref[...], qt[base + PEER_HALF:base + 2 * PEER_HALF].astype(BF16),
                         preferred_element_type=F32)
            v1 = _top_values_sorted(s1, _NTOP)
            v2 = _top_values_sorted(s2, _NTOP)
            for n, (a, b) in enumerate(_CAND):
                cand_scr[n:n + 1, :] = v1[a] + v2[b]
            cand = cand_scr[...]
            top = _top_values_sorted(cand, _NTOP)
            tau = 0.5 * (top[PEER_TOPK - 1] + top[PEER_TOPK])
            cmax = top[0]
            z = jnp.sum(jnp.where(cand >= tau, jnp.exp(cand - cmax), 0.0), axis=0, keepdims=True)
            logz = cmax + jnp.log(z)
            thr = (logz - s1) * LOG2E + 1.0
            for lt in range(tm // LANE):
                thr_scr[h, lt] = thr[:, lt * LANE:(lt + 1) * LANE]
            s2_scr[h] = s2 * LOG2E
            kap_scr[h] = (tau - logz) * LOG2E - 1.0
        acc_scr[...] = jnp.zeros(acc_scr.shape, F32)

    slot = c % 2
    nal = ec // N_KEYS
    kb_rows = N_KEYS // 2

    def stage_a():
        h_scr[slot] = jnp.dot(u_ref[...], xt_scr[...], preferred_element_type=F32)

    def stage_b():
        for lt in range(tm // LANE):
            cols = slice(lt * LANE, (lt + 1) * LANE)
            for half in range(N_KEYS // kb_rows):
                rows = slice(half * kb_rows, (half + 1) * kb_rows)
                w = [jnp.zeros((kb_rows, LANE), F32) for _ in range(nal)]
                for h in range(PEER_HEADS):
                    s2t = s2_scr[h, rows, cols]
                    kap = kap_scr[h, :, cols]
                    for al in range(nal):
                        thr = thr_scr[h, lt, pl.ds((c - 1) * nal + al, 1), :]
                        x = s2t - thr
                        w[al] = w[al] + jnp.where(x >= kap, jnp.exp2(x), 0.0)
                for al in range(nal):
                    r0 = al * N_KEYS + half * kb_rows
                    hv = h_scr[1 - slot, r0:r0 + kb_rows, cols]
                    a = hv * (1.0 + lax.erf(hv * np.float32(math.sqrt(0.5)))) * w[al]
                    a_scr[1 - slot, r0:r0 + kb_rows, cols] = a.astype(BF16)

    def stage_c():
        acc_scr[...] += jnp.dot(vt_ref[...], a_scr[slot], preferred_element_type=F32)

    @pl.when(jnp.logical_and(c >= 2, c < nc))
    def _():
        stage_c()
        stage_b()
        stage_a()

    @pl.when(c == 0)
    def _():
        stage_a()

    @pl.when(c == 1)
    def _():
        stage_b()
        stage_a()

    @pl.when(c == nc)
    def _():
        stage_c()
        stage_b()

    @pl.when(c == nc + 1)
    def _():
        stage_c()


    @pl.when(c == pl.num_programs(1) - 1)
    def _():
        y = r_ref[...] + acc_scr[...].T
        o_ref[...] = _layer_norm(y, g_ref[...], b_ref[...])


def _peer(x1, r, wqt, k1, k2, u, vt, l2g, l2b):
    t = x1.shape[0]
    tm, ec = TM_PEER, EC_PEER
    assert t % tm == 0 and N_EXPERTS % ec == 0 and ec % N_KEYS == 0
    nc = N_EXPERTS // ec
    single = pl.Buffered(1)
    row = lambda i, c: (i, 0)
    const = lambda i, c: (0, 0)
    stat = pltpu.VMEM((PEER_HEADS, N_KEYS, tm), F32)
    return pl.pallas_call(
        functools.partial(_peer_kernel, nc=nc),
        grid=(t // tm, nc + 2),
        in_specs=[pl.BlockSpec((tm, D_MODEL), row, pipeline_mode=single),
                  pl.BlockSpec((tm, D_MODEL), row, pipeline_mode=single),
                  pl.BlockSpec((PEER_HEADS * 2 * PEER_HALF, D_MODEL), const, pipeline_mode=single),
                  pl.BlockSpec((N_KEYS, PEER_HALF), const, pipeline_mode=single),
                  pl.BlockSpec((N_KEYS, PEER_HALF), const, pipeline_mode=single),
                  pl.BlockSpec((ec, D_MODEL), lambda i, c: (jnp.minimum(c, nc - 1), 0)),
                  pl.BlockSpec((None, D_MODEL, ec), lambda i, c: (jnp.clip(c - 2, 0, nc - 1), 0, 0)),
                  pl.BlockSpec((1, D_MODEL), const, pipeline_mode=single),
                  pl.BlockSpec((1, D_MODEL), const, pipeline_mode=single)],
        out_specs=pl.BlockSpec((tm, D_MODEL), row, pipeline_mode=single),
        out_shape=jax.ShapeDtypeStruct((t, D_MODEL), F32),
        scratch_shapes=[pltpu.VMEM((D_MODEL, tm), BF16),
                        pltpu.VMEM((PEER_HEADS, tm // LANE, N_KEYS, LANE), F32), stat,
                        pltpu.VMEM((PEER_HEADS, 1, tm), F32),
                        pltpu.VMEM((_CAND_ROWS, tm), F32),
                        pltpu.VMEM((2, ec, tm), F32), pltpu.VMEM((2, ec, tm), BF16),
                        pltpu.VMEM((D_MODEL, tm), F32)],
        compiler_params=pltpu.CompilerParams(
            dimension_semantics=("arbitrary", "arbitrary"), vmem_limit_bytes=VMEM_LIMIT_BYTES,
        ),
        name="peer",
    )(x1, r, wqt, k1, k2, u, vt, l2g.reshape(1, -1), l2b.reshape(1, -1))


def kernel(x, p, ln_in_g, ln_in_b, w_in, pool_grp_w, pool_scale, lambda_q1, lambda_k1,
           lambda_q2, lambda_k2, subln_g, w_branch_pool, w_branch_attn, gate_b, w_out,
           ln1_g, ln1_b, peer_wq, peer_keys1, peer_keys2, peer_u, peer_v, ple_w, ple_gate_w,
           ln2_g, ln2_b, rel_bias):
    batch, seq, d = x.shape
    assert d == D_MODEL and w_in.shape[0] == DEPTH == 1
    t = batch * seq
    c0, c1 = POOL_WIDTH, POOL_WIDTH + QK_WIDTH
    c2 = c1 + QK_WIDTH
    c3 = c2 + ATTN_WIDTH

    w = w_in[0]
    wq = w[:, c0:c1] * (HEAD_DIM ** -0.5 * LOG2E)
    w_nat = jnp.concatenate([w[:, c3:], w[:, :c0], wq, w[:, c1:c2]], axis=1).astype(BF16)
    w_nat = w_nat.reshape(D_MODEL, NAT_COLS // TN_IN, TN_IN).transpose(1, 0, 2)
    w_vt = w[:, c2:c3].T.astype(BF16)

    bias, lam = _attn_bias(rel_bias, lambda_q1[0], lambda_k1[0], lambda_q2[0], lambda_k2[0])
    hn, x0, vt = _in_proj(x.reshape(t, d), ln_in_g, ln_in_b, w_nat, w_vt, batch, seq)
    ya = _diff_attn(lam, hn, vt, bias, subln_g[0], batch, seq)
    x1, r = _merge(hn, ya, x0, p[0].reshape(t, PLE_DIM),
                   pool_grp_w[0].astype(BF16), pool_scale[0].reshape(1, -1),
                   w_branch_pool[0].astype(BF16), w_branch_attn[0].astype(BF16),
                   gate_b[0].reshape(1, -1), w_out[0].astype(BF16),
                   ln1_g[0].reshape(1, -1), ln1_b[0].reshape(1, -1),
                   ple_gate_w[0].astype(BF16), ple_w[0].astype(BF16), seq)
    out = _peer(x1, r, peer_wq[0].T.astype(BF16), peer_keys1[0].astype(BF16),
                peer_keys2[0].astype(BF16), peer_u[0].astype(BF16),
                peer_v[0].astype(BF16).reshape(N_EXPERTS // EC_PEER, EC_PEER, d).transpose(0, 2, 1),
                ln2_g[0], ln2_b[0])
    return out.reshape(batch, seq, d)
```

```python
import functools
import math

import jax
import jax.numpy as jnp
import numpy as np
from jax import lax
from jax.experimental import pallas as pl
from jax.experimental.pallas import tpu as pltpu

F32 = jnp.float32
BF16 = jnp.bfloat16

D_MODEL = 2048
DEPTH = 1
CHUNK = 64
POOL_WINDOWS = (2, 4, 8, 16)
POOL_GROUPS = 4
POOL_WIDTH = D_MODEL // 2
POOL_GROUP_DIM = POOL_WIDTH // POOL_GROUPS
N_HEADS = 8
HEAD_DIM = D_MODEL // 32
V_DIM = 2 * HEAD_DIM
QK_WIDTH = N_HEADS * 2 * HEAD_DIM
ATTN_WIDTH = N_HEADS * V_DIM
N_BRANCHES = 2
GATE_WIDTH = N_BRANCHES * D_MODEL
NUM_BUCKETS = 32
MAX_DISTANCE = 128
PEER_HEADS = 8
N_KEYS = 128
N_EXPERTS = N_KEYS * N_KEYS
PEER_HALF = 128
PEER_TOPK = 16
PLE_DIM = 256
ALPHA = (2 * DEPTH) ** 0.25
LN_EPS = 1e-5
MASK_VALUE = -1e30
LAM_INIT = 0.8 - 0.6 * math.exp(-0.3 * 0)

LANE = 128
SUBLANE = 8
VMEM_LIMIT_BYTES = 60 * 1024 * 1024

V_AUG = V_DIM + 16
LOG2E = math.log2(math.e)
TQ = 256
TKV = 256
ATTN_UNROLL = 4
HALO = 16
TM_IN = 512
TN_IN = 1024
TM_MERGE = 256
TM_PEER = 512
EC_PEER = 512

COL_G = 0
COL_P = GATE_WIDTH
COL_Q = COL_P + POOL_WIDTH
COL_K = COL_Q + QK_WIDTH
NAT_COLS = COL_K + QK_WIDTH


def _layer_norm(x, g, b):
    mu = jnp.mean(x, axis=-1, keepdims=True)
    xc = x - mu
    var = jnp.mean(xc * xc, axis=-1, keepdims=True)
    return xc * lax.rsqrt(var + LN_EPS) * g + b


def _sigmoid(x):
    return 1.0 / (1.0 + jnp.exp(-x))


def _t5_bucket(rel):
    nb = NUM_BUCKETS // 2
    ret = (rel > 0).astype(jnp.int32) * nb
    n = jnp.abs(rel)
    max_exact = nb // 2
    is_small = n < max_exact
    nf = jnp.maximum(n, 1).astype(F32)
    large = max_exact + (jnp.log(nf / max_exact) / math.log(MAX_DISTANCE / max_exact)
                         * (nb - max_exact)).astype(jnp.int32)
    large = jnp.minimum(large, nb - 1)
    return ret + jnp.where(is_small, n, large)


def _bias_kernel(tab_ref, lq1_ref, lk1_ref, lq2_ref, lk2_ref, bias_ref, lam_ref):
    h = pl.program_id(0)
    shape = (2 * TKV, 2 * TQ)
    c = lax.broadcasted_iota(jnp.int32, shape, 0)
    r = lax.broadcasted_iota(jnp.int32, shape, 1)
    r = jnp.where(r >= TQ, r - TQ, r)
    rel = c - TKV - r
    bucket = _t5_bucket(rel)
    far = tab_ref[h, NUM_BUCKETS // 2 - 1]
    val = jnp.zeros(shape, F32)
    for b in range(NUM_BUCKETS):
        val = jnp.where(bucket == b, (tab_ref[h, b] - far) * LOG2E, val)
    shift = CHUNK.bit_length() - 1
    allowed = lax.shift_right_logical(c, shift) <= lax.shift_right_logical(r, shift) + TKV // CHUNK
    bias_ref[0:TKV, :] = jnp.zeros((TKV, 2 * TQ), F32)
    bias_ref[TKV:3 * TKV, :] = jnp.where(allowed, val, MASK_VALUE)
    lam = (jnp.exp(jnp.sum(lq1_ref[...] * lk1_ref[...], axis=-1, keepdims=True))
           - jnp.exp(jnp.sum(lq2_ref[...] * lk2_ref[...], axis=-1, keepdims=True)) + LAM_INIT)
    lam_ref[...] = jnp.broadcast_to(lam, lam_ref.shape)


def _attn_bias(rel_bias, lq1, lk1, lq2, lk2):
    assert TQ == TKV and TQ > MAX_DISTANCE and TQ % CHUNK == 0
    tab = rel_bias.T.astype(F32)
    vec = pl.BlockSpec((1, HEAD_DIM), lambda h: (0, 0))
    return pl.pallas_call(
        _bias_kernel,
        grid=(N_HEADS,),
        in_specs=[pl.BlockSpec(memory_space=pltpu.SMEM), vec, vec, vec, vec],
        out_specs=[pl.BlockSpec((None, 3 * TKV, 2 * TQ), lambda h: (h, 0, 0)),
                   pl.BlockSpec((None, SUBLANE, LANE), lambda h: (h, 0, 0))],
        out_shape=[jax.ShapeDtypeStruct((N_HEADS, 3 * TKV, 2 * TQ), F32),
                   jax.ShapeDtypeStruct((N_HEADS, SUBLANE, LANE), F32)],
        name="attn_bias",
    )(tab, lq1.reshape(1, -1), lk1.reshape(1, -1), lq2.reshape(1, -1), lk2.reshape(1, -1))


def _inproj_kernel(x_ref, g_ref, b_ref, w_ref, wvt_ref, hn_ref, x0_ref, vt_ref, xb_scr):
    j = pl.program_id(1)

    @pl.when(j == 0)
    def _():
        y = _layer_norm(x_ref[...], g_ref[...], b_ref[...])
        x0_ref[...] = y
        yb = y.astype(BF16)
        xb_scr[...] = yb
        vt = lax.dot_general(wvt_ref[...], yb, (((1,), (1,)), ((), ())),
                             preferred_element_type=F32)
        row = lax.broadcasted_iota(jnp.int32, (V_AUG - V_DIM, TKV), 0)
        ones_row = jnp.where(row == 0, 1.0, 0.0).astype(BF16)
        for s in range(vt_ref.shape[0]):
            for h in range(N_HEADS):
                vt_ref[s, h, 0:V_DIM, :] = vt[h * V_DIM:(h + 1) * V_DIM,
                                              s * TKV:(s + 1) * TKV].astype(BF16)
                vt_ref[s, h, V_DIM:V_AUG, :] = ones_row

    hn_ref[...] = jnp.dot(xb_scr[...], w_ref[...], preferred_element_type=F32).astype(BF16)


def _in_proj(x2, ln_g, ln_b, w_nat, w_vt, batch, seq):
    t = x2.shape[0]
    tm, tn = TM_IN, TN_IN
    assert seq % tm == 0 and tm % TKV == 0 and NAT_COLS % tn == 0
    nblk = seq // tm
    const = lambda i, j: (0, 0)
    return pl.pallas_call(
        _inproj_kernel,
        grid=(t // tm, NAT_COLS // tn),
        in_specs=[pl.BlockSpec((tm, D_MODEL), lambda i, j: (i, 0)),
                  pl.BlockSpec((1, D_MODEL), const),
                  pl.BlockSpec((1, D_MODEL), const),
                  pl.BlockSpec((None, D_MODEL, tn), lambda i, j: (j, 0, 0)),
                  pl.BlockSpec((ATTN_WIDTH, D_MODEL), const)],
        out_specs=[pl.BlockSpec((tm, tn), lambda i, j: (i, j)),
                   pl.BlockSpec((tm, D_MODEL), lambda i, j: (i, 0)),
                   pl.BlockSpec((None, tm // TKV, N_HEADS, V_AUG, TKV),
                                lambda i, j: (i // nblk, i % nblk, 0, 0, 0))],
        out_shape=[jax.ShapeDtypeStruct((t, NAT_COLS), BF16),
                   jax.ShapeDtypeStruct((t, D_MODEL), F32),
                   jax.ShapeDtypeStruct((batch, seq // TKV, N_HEADS, V_AUG, TKV), BF16)],
        scratch_shapes=[pltpu.VMEM((tm, D_MODEL), BF16)],
        compiler_params=pltpu.CompilerParams(
            dimension_semantics=("arbitrary", "arbitrary"), vmem_limit_bytes=VMEM_LIMIT_BYTES),
        name="in_proj",
    )(x2, ln_g.reshape(1, -1), ln_b.reshape(1, -1), w_nat, w_vt)


def _attn_kernel(lam_ref, q_ref, k_ref, vt_ref, bias_ref, g_ref, o_ref,
                 q2_scr, s_scr, p_scr, al_scr, m_scr, acc_scr, *, nq):
    npairs = nq * (nq + 1) // 2
    lane = lax.broadcasted_iota(jnp.int32, (TQ, LANE), 1)

    def prep(i, carry):
        qs = q_ref[pl.ds(pl.multiple_of(i * TQ, TQ), TQ), :].astype(F32)
        q2_scr[i] = jnp.concatenate([jnp.where(lane < HEAD_DIM, qs, 0.0),
                                     jnp.where(lane >= HEAD_DIM, qs, 0.0)], axis=0).astype(BF16)
        m_scr[i] = jnp.full(m_scr.shape[1:], -jnp.inf, F32)
        acc_scr[i] = jnp.zeros(acc_scr.shape[1:], F32)
        return carry

    lax.fori_loop(0, nq, prep, 0)
    s_scr[1] = jnp.zeros(s_scr.shape[1:], F32)
    p_scr[0] = jnp.zeros(p_scr.shape[1:], BF16)
    al_scr[0] = jnp.ones(al_scr.shape[1:], F32)

    def advance(i, j):
        wrap = j >= i
        last = jnp.logical_and(i == nq - 1, wrap)
        ni = jnp.where(jnp.logical_and(wrap, jnp.logical_not(last)), i + 1, i)
        nj = jnp.where(last, j, jnp.where(wrap, 0, j + 1))
        return ni, nj

    def substep(t, sa, pairs):
        (ia, ja), (ib, jb), (ic, jc) = pairs
        sb = 1 - sa
        pv = jnp.dot(vt_ref[jc], p_scr[sa], preferred_element_type=F32)
        acc_scr[ic] = acc_scr[ic] * al_scr[sa] + pv
        live_b = jnp.logical_and(t >= 1, t <= npairs)
        st = s_scr[sb]
        m_prev = m_scr[ib]
        m_new = jnp.where(live_b, jnp.maximum(m_prev, jnp.max(st, axis=0, keepdims=True)), m_prev)
        al_scr[sb] = jnp.where(live_b, jnp.exp2(m_prev - m_new), 1.0)
        m_scr[ib] = m_new
        p_scr[sb] = jnp.exp2(st - jnp.where(live_b, m_new, jnp.inf)).astype(BF16)
        kblk = k_ref[pl.ds(pl.multiple_of(ja * TKV, TKV), TKV), :]
        boff = jnp.clip(ja - (ia - 2), 0, 2) * TKV
        s_scr[sa] = lax.dot_general(kblk, q2_scr[ia], (((1,), (1,)), ((), ())),
                                    preferred_element_type=F32) + bias_ref[pl.ds(pl.multiple_of(boff, TKV), TKV), :]
        return (advance(ia, ja), (ia, ja), (ib, jb))

    def body(tt, pairs):
        for u in range(ATTN_UNROLL):
            pairs = substep(ATTN_UNROLL * tt + u, u % 2, pairs)
        return pairs

    zero = jnp.int32(0)
    lax.fori_loop(0, pl.cdiv(npairs + 2, ATTN_UNROLL), body, ((zero, zero), (zero, zero), (zero, zero)))

    lam = lam_ref[0:1, 0:1]

    def finish(i, carry):
        acc = acc_scr[i]
        o = acc[0:V_DIM, :] / acc[V_DIM:V_DIM + 1, :]
        att = o[:, :TQ] - lam * o[:, TQ:]
        ms = jnp.mean(att * att, axis=0, keepdims=True)
        y = att * lax.rsqrt(ms + LN_EPS) * g_ref[...] * (1.0 - LAM_INIT)
        o_ref[pl.ds(pl.multiple_of(i * TQ, TQ), TQ), :] = y.T.astype(BF16)
        return carry

    lax.fori_loop(0, nq, finish, 0)


def _diff_attn(lam, hn, vt, bias, subln_g, batch, seq):
    t = hn.shape[0]
    nq = seq // TQ
    qb, kb = COL_Q // LANE, COL_K // LANE
    return pl.pallas_call(
        functools.partial(_attn_kernel, nq=nq),
        grid=(batch, N_HEADS),
        in_specs=[pl.BlockSpec((None, SUBLANE, LANE), lambda b, h: (h, 0, 0)),
                  pl.BlockSpec((seq, LANE), lambda b, h: (b, qb + h)),
                  pl.BlockSpec((seq, LANE), lambda b, h: (b, kb + h)),
                  pl.BlockSpec((None, seq // TKV, None, V_AUG, TKV), lambda b, h: (b, 0, h, 0, 0)),
                  pl.BlockSpec((None, 3 * TKV, 2 * TQ), lambda b, h: (h, 0, 0)),
                  pl.BlockSpec((V_DIM, 1), lambda b, h: (0, 0))],
        out_specs=pl.BlockSpec((seq, V_DIM), lambda b, h: (b, h)),
        out_shape=jax.ShapeDtypeStruct((t, ATTN_WIDTH), BF16),
        scratch_shapes=[pltpu.VMEM((nq, 2 * TQ, LANE), BF16),
                        pltpu.VMEM((2, TKV, 2 * TQ), F32), pltpu.VMEM((2, TKV, 2 * TQ), BF16),
                        pltpu.VMEM((2, 1, 2 * TQ), F32),
                        pltpu.VMEM((nq, 1, 2 * TQ), F32),
                        pltpu.VMEM((nq, V_AUG, 2 * TQ), F32)],
        compiler_params=pltpu.CompilerParams(
            dimension_semantics=("arbitrary", "arbitrary"),
            vmem_limit_bytes=VMEM_LIMIT_BYTES),
        name="diff_attn",
    )(lam, hn, hn, vt, bias, subln_g.reshape(-1, 1))


def _merge_kernel(gp_ref, xp_ref, halo_ref, ya_ref, x0_ref, p_ref, wgrp_ref, psc_ref, wbp_ref,
                  wba_ref, gb_ref, wout_ref, l1g_ref, l1b_ref, wg_ref, wp_ref,
                  x1_ref, r_ref, *, nblk):
    i = pl.program_id(0)
    tm = xp_ref.shape[0]
    blk_in_seq = i % nblk
    xp = xp_ref[...].astype(F32)
    halo = halo_ref[...].astype(F32)
    halo = jnp.where(blk_in_seq == 0, jnp.zeros_like(halo), halo)
    ext = jnp.concatenate([halo, xp], axis=0)
    pos = blk_in_seq * tm + lax.broadcasted_iota(jnp.int32, (tm, 1), 0)

    yps = []
    for gi, w in enumerate(POOL_WINDOWS):
        lo, hi = gi * POOL_GROUP_DIM, (gi + 1) * POOL_GROUP_DIM
        s = ext[:, lo:hi]
        span = 1
        while span < w:
            s = s[span:] + s[:-span]
            span *= 2
        start = HALO - (w - 1)
        win = s[start:start + tm]
        cnt = jnp.minimum(pos + 1, w).astype(F32)
        mixed = win / cnt - xp[:, lo:hi]
        yps.append(jnp.dot(mixed.astype(BF16), wgrp_ref[gi], preferred_element_type=F32))
    yp = jnp.concatenate(yps, axis=1) * psc_ref[...]

    gates = _sigmoid(gp_ref[...].astype(F32) + gb_ref[...])
    bp = jnp.dot(yp.astype(BF16), wbp_ref[...], preferred_element_type=F32)
    ba = jnp.dot(ya_ref[...], wba_ref[...], preferred_element_type=F32)
    merged = gates[:, :D_MODEL] * bp + gates[:, D_MODEL:] * ba
    y = jnp.dot(merged.astype(BF16), wout_ref[...], preferred_element_type=F32)
    x1 = _layer_norm(ALPHA * x0_ref[...] + y, l1g_ref[...], l1b_ref[...])
    x1_ref[...] = x1

    x1b = x1.astype(BF16)
    e = (_sigmoid(jnp.dot(x1b, wg_ref[...], preferred_element_type=F32))
         * jnp.dot(p_ref[...].astype(BF16), wp_ref[...], preferred_element_type=F32))
    r_ref[...] = ALPHA * x1 + e


def _merge(hn, ya, x0, p2, wgrp, psc, wbp, wba, gate_b, wout, l1g, l1b, wg, wp, seq):
    t = hn.shape[0]
    tm = TM_MERGE
    assert seq % tm == 0 and tm % HALO == 0
    nblk = seq // tm
    const2 = lambda i: (0, 0)
    single = pl.Buffered(1)
    wspec = lambda shape: pl.BlockSpec(shape, (lambda i: (0,) * len(shape)), pipeline_mode=single)
    return pl.pallas_call(
        functools.partial(_merge_kernel, nblk=nblk),
        grid=(t // tm,),
        in_specs=[pl.BlockSpec((tm, GATE_WIDTH), lambda i: (i, COL_G // GATE_WIDTH)),
                  pl.BlockSpec((tm, POOL_WIDTH), lambda i: (i, COL_P // POOL_WIDTH)),
                  pl.BlockSpec((HALO, POOL_WIDTH),
                               lambda i: (jnp.maximum(i * (tm // HALO) - 1, 0), COL_P // POOL_WIDTH)),
                  pl.BlockSpec((tm, ATTN_WIDTH), lambda i: (i, 0)),
                  pl.BlockSpec((tm, D_MODEL), lambda i: (i, 0)),
                  pl.BlockSpec((tm, PLE_DIM), lambda i: (i, 0)),
                  wspec((POOL_GROUPS, POOL_GROUP_DIM, POOL_GROUP_DIM)),
                  wspec((1, POOL_WIDTH)),
                  wspec((POOL_WIDTH, D_MODEL)),
                  wspec((ATTN_WIDTH, D_MODEL)),
                  wspec((1, GATE_WIDTH)),
                  wspec((D_MODEL, D_MODEL)),
                  wspec((1, D_MODEL)),
                  wspec((1, D_MODEL)),
                  wspec((D_MODEL, D_MODEL)),
                  wspec((PLE_DIM, D_MODEL))],
        out_specs=[pl.BlockSpec((tm, D_MODEL), lambda i: (i, 0)),
                   pl.BlockSpec((tm, D_MODEL), lambda i: (i, 0))],
        out_shape=[jax.ShapeDtypeStruct((t, D_MODEL), F32),
                   jax.ShapeDtypeStruct((t, D_MODEL), F32)],
        compiler_params=pltpu.CompilerParams(
            dimension_semantics=("arbitrary",), vmem_limit_bytes=VMEM_LIMIT_BYTES),
        name="merge",
    )(hn, hn, hn, ya, x0, p2, wgrp, psc, wbp, wba, gate_b, wout, l1g, l1b, wg, wp)


_NTOP = PEER_TOPK + 1
_CAND = [(a, b) for a in range(_NTOP) for b in range(_NTOP)
         if (a + 1) * (b + 1) <= _NTOP]
_CAND_ROWS = -(-len(_CAND) // SUBLANE) * SUBLANE


def _sorting_network(n):
    pairs = []
    p = 1
    while p < n:
        k = p
        while k >= 1:
            for j in range(k % p, n - k, 2 * k):
                for i in range(min(k, n - j - k)):
                    if (i + j) // (2 * p) == (i + j + k) // (2 * p):
                        pairs.append((i + j, i + j + k))
            k //= 2
        p *= 2
    return pairs


def _top_values_sorted(s, k):
    ng = s.shape[0] // SUBLANE
    wires = 1 << (ng - 1).bit_length()
    v = [s[g * SUBLANE:(g + 1) * SUBLANE, :] for g in range(ng)]
    v += [jnp.full_like(v[0], -jnp.inf)] * (wires - ng)
    for i, j in _sorting_network(wires):
        v[i], v[j] = jnp.maximum(v[i], v[j]), jnp.minimum(v[i], v[j])
    v = v[:ng]
    vals = []
    for r in range(k):
        m = jnp.max(v[0], axis=0, keepdims=True)
        vals.append(m)
        left = k - 1 - r
        if left == 0:
            break
        hit = v[0] == m
        for d in range(min(left, ng - 1)):
            v[d] = jnp.where(hit, v[d + 1], v[d])
        if left >= ng:
            v[ng - 1] = jnp.where(hit, -jnp.inf, v[ng - 1])
    return vals


def _peer_kernel(x1_ref, r_ref, wqt_ref, k1_ref, k2_ref, u_ref, vt_ref, g_ref, b_ref, o_ref,
                 xt_scr, e1_scr, e2_scr, kp_scr, cand_scr, h_scr, a_scr, acc_scr, *, nc):
    c = pl.program_id(1)
    tm = x1_ref.shape[0]
    ec = u_ref.shape[0]

    @pl.when(c == 0)
    def _():
        xt = x1_ref[...].T.astype(BF16)
        xt_scr[...] = xt
        cand_scr[...] = jnp.full(cand_scr.shape, -jnp.inf, F32)
        qt = jnp.dot(wqt_ref[...], xt, preferred_element_type=F32)
        for h in range(PEER_HEADS):
            base = h * 2 * PEER_HALF
            s1 = jnp.dot(k1_ref[...], qt[base:base + PEER_HALF].astype(BF16),
                         preferred_element_type=F32)
            s2 = jnp.dot(k2_ref[...], qt[base + PEER_HALF:base + 2 * PEER_HALF].astype(BF16),
                         preferred_element_type=F32)
            v1 = _top_values_sorted(s1, _NTOP)
            v2 = _top_values_sorted(s2, _NTOP)
            for n, (a, b) in enumerate(_CAND):
                cand_scr[n:n + 1, :] = v1[a] + v2[b]
            cand = cand_scr[...]
            top = _top_values_sorted(cand, _NTOP)
            tau = 0.5 * (top[PEER_TOPK - 1] + top[PEER_TOPK])
            cmax = top[0]
            z = jnp.sum(jnp.where(cand >= tau, jnp.exp(cand - cmax), 0.0), axis=0, keepdims=True)
            logz = cmax + jnp.log(z)
            e1 = jnp.exp2((s1 + v2[0] - logz) * LOG2E - 1.0)
            for lt in range(tm // LANE):
                e1_scr[h, lt] = e1[:, lt * LANE:(lt + 1) * LANE]
            e2_scr[h] = jnp.exp2((s2 - v2[0]) * LOG2E)
            kp_scr[h] = jnp.exp2((tau - logz) * LOG2E - 1.0)
        acc_scr[...] = jnp.zeros(acc_scr.shape, F32)

    slot = c % 2
    nal = ec // N_KEYS
    kb_rows = N_KEYS // 2

    def stage_a():
        h_scr[slot] = jnp.dot(u_ref[...], xt_scr[...], preferred_element_type=F32)

    def stage_b():
        for lt in range(tm // LANE):
            cols = slice(lt * LANE, (lt + 1) * LANE)
            for half in range(N_KEYS // kb_rows):
                rows = slice(half * kb_rows, (half + 1) * kb_rows)
                w = [jnp.zeros((kb_rows, LANE), F32) for _ in range(nal)]
                for h in range(PEER_HEADS):
                    e2t = e2_scr[h, rows, cols]
                    kp = kp_scr[h, :, cols]
                    for al in range(nal):
                        e1 = e1_scr[h, lt, pl.ds((c - 1) * nal + al, 1), :]
                        e = e2t * e1
                        w[al] = w[al] + jnp.where(e >= kp, e, 0.0)
                for al in range(nal):
                    r0 = al * N_KEYS + half * kb_rows
                    hv = h_scr[1 - slot, r0:r0 + kb_rows, cols]
                    a = hv * (1.0 + lax.erf(hv * np.float32(math.sqrt(0.5)))) * w[al]
                    a_scr[1 - slot, r0:r0 + kb_rows, cols] = a.astype(BF16)

    def stage_c():
        acc_scr[...] += jnp.dot(vt_ref[...], a_scr[slot], preferred_element_type=F32)

    @pl.when(jnp.logical_and(c >= 2, c < nc))
    def _():
        stage_c()
        stage_b()
        stage_a()

    @pl.when(c == 0)
    def _():
        stage_a()

    @pl.when(c == 1)
    def _():
        stage_b()
        stage_a()

    @pl.when(c == nc)
    def _():
        stage_c()
        stage_b()

    @pl.when(c == nc + 1)
    def _():
        stage_c()


    @pl.when(c == pl.num_programs(1) - 1)
    def _():
        y = r_ref[...] + acc_scr[...].T
        o_ref[...] = _layer_norm(y, g_ref[...], b_ref[...])


def _peer(x1, r, wqt, k1, k2, u, vt, l2g, l2b):
    t = x1.shape[0]
    tm, ec = TM_PEER, EC_PEER
    assert t % tm == 0 and N_EXPERTS % ec == 0 and ec % N_KEYS == 0
    nc = N_EXPERTS // ec
    single = pl.Buffered(1)
    row = lambda i, c: (i, 0)
    const = lambda i, c: (0, 0)
    stat = pltpu.VMEM((PEER_HEADS, N_KEYS, tm), F32)
    return pl.pallas_call(
        functools.partial(_peer_kernel, nc=nc),
        grid=(t // tm, nc + 2),
        in_specs=[pl.BlockSpec((tm, D_MODEL), row, pipeline_mode=single),
                  pl.BlockSpec((tm, D_MODEL), row, pipeline_mode=single),
                  pl.BlockSpec((PEER_HEADS * 2 * PEER_HALF, D_MODEL), const, pipeline_mode=single),
                  pl.BlockSpec((N_KEYS, PEER_HALF), const, pipeline_mode=single),
                  pl.BlockSpec((N_KEYS, PEER_HALF), const, pipeline_mode=single),
                  pl.BlockSpec((ec, D_MODEL), lambda i, c: (jnp.minimum(c, nc - 1), 0)),
                  pl.BlockSpec((None, D_MODEL, ec), lambda i, c: (jnp.clip(c - 2, 0, nc - 1), 0, 0)),
                  pl.BlockSpec((1, D_MODEL), const, pipeline_mode=single),
                  pl.BlockSpec((1, D_MODEL), const, pipeline_mode=single)],
        out_specs=pl.BlockSpec((tm, D_MODEL), row, pipeline_mode=single),
        out_shape=jax.ShapeDtypeStruct((t, D_MODEL), F32),
        scratch_shapes=[pltpu.VMEM((D_MODEL, tm), BF16),
                        pltpu.VMEM((PEER_HEADS, tm // LANE, N_KEYS, LANE), F32), stat,
                        pltpu.VMEM((PEER_HEADS, 1, tm), F32),
                        pltpu.VMEM((_CAND_ROWS, tm), F32),
                        pltpu.VMEM((2, ec, tm), F32), pltpu.VMEM((2, ec, tm), BF16),
                        pltpu.VMEM((D_MODEL, tm), F32)],
        compiler_params=pltpu.CompilerParams(
            dimension_semantics=("arbitrary", "arbitrary"), vmem_limit_bytes=VMEM_LIMIT_BYTES,
        ),
        name="peer",
    )(x1, r, wqt, k1, k2, u, vt, l2g.reshape(1, -1), l2b.reshape(1, -1))


def kernel(x, p, ln_in_g, ln_in_b, w_in, pool_grp_w, pool_scale, lambda_q1, lambda_k1,
           lambda_q2, lambda_k2, subln_g, w_branch_pool, w_branch_attn, gate_b, w_out,
           ln1_g, ln1_b, peer_wq, peer_keys1, peer_keys2, peer_u, peer_v, ple_w, ple_gate_w,
           ln2_g, ln2_b, rel_bias):
    batch, seq, d = x.shape
    assert d == D_MODEL and w_in.shape[0] == DEPTH == 1
    t = batch * seq
    c0, c1 = POOL_WIDTH, POOL_WIDTH + QK_WIDTH
    c2 = c1 + QK_WIDTH
    c3 = c2 + ATTN_WIDTH

    w = w_in[0]
    wq = w[:, c0:c1] * (HEAD_DIM ** -0.5 * LOG2E)
    w_nat = jnp.concatenate([w[:, c3:], w[:, :c0], wq, w[:, c1:c2]], axis=1).astype(BF16)
    w_nat = w_nat.reshape(D_MODEL, NAT_COLS // TN_IN, TN_IN).transpose(1, 0, 2)
    w_vt = w[:, c2:c3].T.astype(BF16)

    bias, lam = _attn_bias(rel_bias, lambda_q1[0], lambda_k1[0], lambda_q2[0], lambda_k2[0])
    hn, x0, vt = _in_proj(x.reshape(t, d), ln_in_g, ln_in_b, w_nat, w_vt, batch, seq)
    ya = _diff_attn(lam, hn, vt, bias, subln_g[0], batch, seq)
    x1, r = _merge(hn, ya, x0, p[0].reshape(t, PLE_DIM),
                   pool_grp_w[0].astype(BF16), pool_scale[0].reshape(1, -1),
                   w_branch_pool[0].astype(BF16), w_branch_attn[0].astype(BF16),
                   gate_b[0].reshape(1, -1), w_out[0].astype(BF16),
                   ln1_g[0].reshape(1, -1), ln1_b[0].reshape(1, -1),
                   ple_gate_w[0].astype(BF16), ple_w[0].astype(BF16), seq)
    out = _peer(x1, r, peer_wq[0].T.astype(BF16), peer_keys1[0].astype(BF16),
                peer_keys2[0].astype(BF16), peer_u[0].astype(BF16),
                peer_v[0].astype(BF16).reshape(N_EXPERTS // EC_PEER, EC_PEER, d).transpose(0, 2, 1),
                ln2_g[0], ln2_b[0])
    return out.reshape(batch, seq, d)
```

```python
import functools
import math

import jax
import jax.numpy as jnp
import numpy as np
from jax import lax
from jax.experimental import pallas as pl
from jax.experimental.pallas import tpu as pltpu

F32 = jnp.float32
BF16 = jnp.bfloat16

D_MODEL = 2048
DEPTH = 1
CHUNK = 64
POOL_WINDOWS = (2, 4, 8, 16)
POOL_GROUPS = 4
POOL_WIDTH = D_MODEL // 2
POOL_GROUP_DIM = POOL_WIDTH // POOL_GROUPS
N_HEADS = 8
HEAD_DIM = D_MODEL // 32
V_DIM = 2 * HEAD_DIM
QK_WIDTH = N_HEADS * 2 * HEAD_DIM
ATTN_WIDTH = N_HEADS * V_DIM
N_BRANCHES = 2
GATE_WIDTH = N_BRANCHES * D_MODEL
NUM_BUCKETS = 32
MAX_DISTANCE = 128
PEER_HEADS = 8
N_KEYS = 128
N_EXPERTS = N_KEYS * N_KEYS
PEER_HALF = 128
PEER_TOPK = 16
PLE_DIM = 256
ALPHA = (2 * DEPTH) ** 0.25
LN_EPS = 1e-5
MASK_VALUE = -1e30
LAM_INIT = 0.8 - 0.6 * math.exp(-0.3 * 0)

LANE = 128
SUBLANE = 8
VMEM_LIMIT_BYTES = 60 * 1024 * 1024

V_AUG = V_DIM + 16
LOG2E = math.log2(math.e)
TQ = 256
TKV = 256
ATTN_UNROLL = 8
HALO = 16
TM_IN = 512
TN_IN = 1024
TM_MERGE = 256
TM_PEER = 512
EC_PEER = 512

COL_G = 0
COL_P = GATE_WIDTH
COL_Q = COL_P + POOL_WIDTH
COL_K = COL_Q + QK_WIDTH
NAT_COLS = COL_K + QK_WIDTH


def _layer_norm(x, g, b):
    mu = jnp.mean(x, axis=-1, keepdims=True)
    xc = x - mu
    var = jnp.mean(xc * xc, axis=-1, keepdims=True)
    return xc * lax.rsqrt(var + LN_EPS) * g + b


def _sigmoid(x):
    return 1.0 / (1.0 + jnp.exp(-x))


def _t5_bucket(rel):
    nb = NUM_BUCKETS // 2
    ret = (rel > 0).astype(jnp.int32) * nb
    n = jnp.abs(rel)
    max_exact = nb // 2
    is_small = n < max_exact
    nf = jnp.maximum(n, 1).astype(F32)
    large = max_exact + (jnp.log(nf / max_exact) / math.log(MAX_DISTANCE / max_exact)
                         * (nb - max_exact)).astype(jnp.int32)
    large = jnp.minimum(large, nb - 1)
    return ret + jnp.where(is_small, n, large)


def _bias_kernel(tab_ref, lq1_ref, lk1_ref, lq2_ref, lk2_ref, bias_ref, lam_ref):
    h = pl.program_id(0)
    shape = (2 * TKV, 2 * TQ)
    c = lax.broadcasted_iota(jnp.int32, shape, 0)
    r = lax.broadcasted_iota(jnp.int32, shape, 1)
    r = jnp.where(r >= TQ, r - TQ, r)
    rel = c - TKV - r
    bucket = _t5_bucket(rel)
    far = tab_ref[h, NUM_BUCKETS // 2 - 1]
    val = jnp.zeros(shape, F32)
    for b in range(NUM_BUCKETS):
        val = jnp.where(bucket == b, (tab_ref[h, b] - far) * LOG2E, val)
    shift = CHUNK.bit_length() - 1
    allowed = lax.shift_right_logical(c, shift) <= lax.shift_right_logical(r, shift) + TKV // CHUNK
    bias_ref[0:TKV, :] = jnp.zeros((TKV, 2 * TQ), F32)
    bias_ref[TKV:3 * TKV, :] = jnp.where(allowed, val, MASK_VALUE)
    lam = (jnp.exp(jnp.sum(lq1_ref[...] * lk1_ref[...], axis=-1, keepdims=True))
           - jnp.exp(jnp.sum(lq2_ref[...] * lk2_ref[...], axis=-1, keepdims=True)) + LAM_INIT)
    lam_ref[...] = jnp.broadcast_to(lam, lam_ref.shape)


def _attn_bias(rel_bias, lq1, lk1, lq2, lk2):
    assert TQ == TKV and TQ > MAX_DISTANCE and TQ % CHUNK == 0
    tab = rel_bias.T.astype(F32)
    vec = pl.BlockSpec((1, HEAD_DIM), lambda h: (0, 0))
    return pl.pallas_call(
        _bias_kernel,
        grid=(N_HEADS,),
        in_specs=[pl.BlockSpec(memory_space=pltpu.SMEM), vec, vec, vec, vec],
        out_specs=[pl.BlockSpec((None, 3 * TKV, 2 * TQ), lambda h: (h, 0, 0)),
                   pl.BlockSpec((None, SUBLANE, LANE), lambda h: (h, 0, 0))],
        out_shape=[jax.ShapeDtypeStruct((N_HEADS, 3 * TKV, 2 * TQ), F32),
                   jax.ShapeDtypeStruct((N_HEADS, SUBLANE, LANE), F32)],
        name="attn_bias",
    )(tab, lq1.reshape(1, -1), lk1.reshape(1, -1), lq2.reshape(1, -1), lk2.reshape(1, -1))


def _inproj_kernel(x_ref, g_ref, b_ref, w_ref, wvt_ref, hn_ref, x0_ref, vt_ref, xb_scr):
    j = pl.program_id(1)

    @pl.when(j == 0)
    def _():
        y = _layer_norm(x_ref[...], g_ref[...], b_ref[...])
        x0_ref[...] = y
        yb = y.astype(BF16)
        xb_scr[...] = yb
        vt = lax.dot_general(wvt_ref[...], yb, (((1,), (1,)), ((), ())),
                             preferred_element_type=F32)
        row = lax.broadcasted_iota(jnp.int32, (V_AUG - V_DIM, TKV), 0)
        ones_row = jnp.where(row == 0, 1.0, 0.0).astype(BF16)
        for s in range(vt_ref.shape[0]):
            for h in range(N_HEADS):
                vt_ref[s, h, 0:V_DIM, :] = vt[h * V_DIM:(h + 1) * V_DIM,
                                              s * TKV:(s + 1) * TKV].astype(BF16)
                vt_ref[s, h, V_DIM:V_AUG, :] = ones_row

    hn_ref[...] = jnp.dot(xb_scr[...], w_ref[...], preferred_element_type=F32).astype(BF16)


def _in_proj(x2, ln_g, ln_b, w_nat, w_vt, batch, seq):
    t = x2.shape[0]
    tm, tn = TM_IN, TN_IN
    assert seq % tm == 0 and tm % TKV == 0 and NAT_COLS % tn == 0
    nblk = seq // tm
    const = lambda i, j: (0, 0)
    return pl.pallas_call(
        _inproj_kernel,
        grid=(t // tm, NAT_COLS // tn),
        in_specs=[pl.BlockSpec((tm, D_MODEL), lambda i, j: (i, 0)),
                  pl.BlockSpec((1, D_MODEL), const),
                  pl.BlockSpec((1, D_MODEL), const),
                  pl.BlockSpec((None, D_MODEL, tn), lambda i, j: (j, 0, 0)),
                  pl.BlockSpec((ATTN_WIDTH, D_MODEL), const)],
        out_specs=[pl.BlockSpec((tm, tn), lambda i, j: (i, j)),
                   pl.BlockSpec((tm, D_MODEL), lambda i, j: (i, 0)),
                   pl.BlockSpec((None, tm // TKV, N_HEADS, V_AUG, TKV),
                                lambda i, j: (i // nblk, i % nblk, 0, 0, 0))],
        out_shape=[jax.ShapeDtypeStruct((t, NAT_COLS), BF16),
                   jax.ShapeDtypeStruct((t, D_MODEL), F32),
                   jax.ShapeDtypeStruct((batch, seq // TKV, N_HEADS, V_AUG, TKV), BF16)],
        scratch_shapes=[pltpu.VMEM((tm, D_MODEL), BF16)],
        compiler_params=pltpu.CompilerParams(
            dimension_semantics=("arbitrary", "arbitrary"), vmem_limit_bytes=VMEM_LIMIT_BYTES),
        name="in_proj",
    )(x2, ln_g.reshape(1, -1), ln_b.reshape(1, -1), w_nat, w_vt)


def _attn_kernel(lam_ref, q_ref, k_ref, vt_ref, bias_ref, g_ref, o_ref,
                 q2_scr, s_scr, p_scr, al_scr, m_scr, acc_scr, *, nq):
    npairs = nq * (nq + 1) // 2
    lane = lax.broadcasted_iota(jnp.int32, (TQ, LANE), 1)

    def prep(i, carry):
        qs = q_ref[pl.ds(pl.multiple_of(i * TQ, TQ), TQ), :].astype(F32)
        q2_scr[i] = jnp.concatenate([jnp.where(lane < HEAD_DIM, qs, 0.0),
                                     jnp.where(lane >= HEAD_DIM, qs, 0.0)], axis=0).astype(BF16)
        m_scr[i] = jnp.full(m_scr.shape[1:], -jnp.inf, F32)
        acc_scr[i] = jnp.zeros(acc_scr.shape[1:], F32)
        return carry

    lax.fori_loop(0, nq, prep, 0)
    s_scr[1] = jnp.zeros(s_scr.shape[1:], F32)
    p_scr[0] = jnp.zeros(p_scr.shape[1:], BF16)
    al_scr[0] = jnp.ones(al_scr.shape[1:], F32)

    def advance(i, j):
        wrap = j >= i
        last = jnp.logical_and(i == nq - 1, wrap)
        ni = jnp.where(jnp.logical_and(wrap, jnp.logical_not(last)), i + 1, i)
        nj = jnp.where(last, j, jnp.where(wrap, 0, j + 1))
        return ni, nj

    def substep(t, sa, pairs):
        (ia, ja), (ib, jb), (ic, jc) = pairs
        sb = 1 - sa
        pv = jnp.dot(vt_ref[jc], p_scr[sa], preferred_element_type=F32)
        acc_scr[ic] = acc_scr[ic] * al_scr[sa] + pv
        live_b = jnp.logical_and(t >= 1, t <= npairs)
        st = s_scr[sb]
        m_prev = m_scr[ib]
        m_new = jnp.where(live_b, jnp.maximum(m_prev, jnp.max(st, axis=0, keepdims=True)), m_prev)
        al_scr[sb] = jnp.where(live_b, jnp.exp2(m_prev - m_new), 1.0)
        m_scr[ib] = m_new
        p_scr[sb] = jnp.exp2(st - jnp.where(live_b, m_new, jnp.inf)).astype(BF16)
        kblk = k_ref[pl.ds(pl.multiple_of(ja * TKV, TKV), TKV), :]
        boff = jnp.clip(ja - (ia - 2), 0, 2) * TKV
        s_scr[sa] = lax.dot_general(kblk, q2_scr[ia], (((1,), (1,)), ((), ())),
                                    preferred_element_type=F32) + bias_ref[pl.ds(pl.multiple_of(boff, TKV), TKV), :]
        return (advance(ia, ja), (ia, ja), (ib, jb))

    def body(tt, pairs):
        for u in range(ATTN_UNROLL):
            pairs = substep(ATTN_UNROLL * tt + u, u % 2, pairs)
        return pairs

    zero = jnp.int32(0)
    lax.fori_loop(0, pl.cdiv(npairs + 2, ATTN_UNROLL), body, ((zero, zero), (zero, zero), (zero, zero)))

    lam = lam_ref[0:1, 0:1]

    def finish(i, carry):
        acc = acc_scr[i]
        o = acc[0:V_DIM, :] / acc[V_DIM:V_DIM + 1, :]
        att = o[:, :TQ] - lam * o[:, TQ:]
        ms = jnp.mean(att * att, axis=0, keepdims=True)
        y = att * lax.rsqrt(ms + LN_EPS) * g_ref[...] * (1.0 - LAM_INIT)
        o_ref[pl.ds(pl.multiple_of(i * TQ, TQ), TQ), :] = y.T.astype(BF16)
        return carry

    lax.fori_loop(0, nq, finish, 0)


def _diff_attn(lam, hn, vt, bias, subln_g, batch, seq):
    t = hn.shape[0]
    nq = seq // TQ
    qb, kb = COL_Q // LANE, COL_K // LANE
    return pl.pallas_call(
        functools.partial(_attn_kernel, nq=nq),
        grid=(batch, N_HEADS),
        in_specs=[pl.BlockSpec((None, SUBLANE, LANE), lambda b, h: (h, 0, 0)),
                  pl.BlockSpec((seq, LANE), lambda b, h: (b, qb + h)),
                  pl.BlockSpec((seq, LANE), lambda b, h: (b, kb + h)),
                  pl.BlockSpec((None, seq // TKV, None, V_AUG, TKV), lambda b, h: (b, 0, h, 0, 0)),
                  pl.BlockSpec((None, 3 * TKV, 2 * TQ), lambda b, h: (h, 0, 0)),
                  pl.BlockSpec((V_DIM, 1), lambda b, h: (0, 0))],
        out_specs=pl.BlockSpec((seq, V_DIM), lambda b, h: (b, h)),
        out_shape=jax.ShapeDtypeStruct((t, ATTN_WIDTH), BF16),
        scratch_shapes=[pltpu.VMEM((nq, 2 * TQ, LANE), BF16),
                        pltpu.VMEM((2, TKV, 2 * TQ), F32), pltpu.VMEM((2, TKV, 2 * TQ), BF16),
                        pltpu.VMEM((2, 1, 2 * TQ), F32),
                        pltpu.VMEM((nq, 1, 2 * TQ), F32),
                        pltpu.VMEM((nq, V_AUG, 2 * TQ), F32)],
        compiler_params=pltpu.CompilerParams(
            dimension_semantics=("arbitrary", "arbitrary"),
            vmem_limit_bytes=VMEM_LIMIT_BYTES),
        name="diff_attn",
    )(lam, hn, hn, vt, bias, subln_g.reshape(-1, 1))


def _merge_kernel(gp_ref, xp_ref, halo_ref, ya_ref, x0_ref, p_ref, wgrp_ref, psc_ref, wbp_ref,
                  wba_ref, gb_ref, wout_ref, l1g_ref, l1b_ref, wg_ref, wp_ref,
                  x1_ref, r_ref, *, nblk):
    i = pl.program_id(0)
    tm = xp_ref.shape[0]
    blk_in_seq = i % nblk
    xp = xp_ref[...].astype(F32)
    halo = halo_ref[...].astype(F32)
    halo = jnp.where(blk_in_seq == 0, jnp.zeros_like(halo), halo)
    ext = jnp.concatenate([halo, xp], axis=0)
    pos = blk_in_seq * tm + lax.broadcasted_iota(jnp.int32, (tm, 1), 0)

    yps = []
    for gi, w in enumerate(POOL_WINDOWS):
        lo, hi = gi * POOL_GROUP_DIM, (gi + 1) * POOL_GROUP_DIM
        s = ext[:, lo:hi]
        span = 1
        while span < w:
            s = s[span:] + s[:-span]
            span *= 2
        start = HALO - (w - 1)
        win = s[start:start + tm]
        cnt = jnp.minimum(pos + 1, w).astype(F32)
        mixed = win / cnt - xp[:, lo:hi]
        yps.append(jnp.dot(mixed.astype(BF16), wgrp_ref[gi], preferred_element_type=F32))
    yp = jnp.concatenate(yps, axis=1) * psc_ref[...]

    gates = _sigmoid(gp_ref[...].astype(F32) + gb_ref[...])
    bp = jnp.dot(yp.astype(BF16), wbp_ref[...], preferred_element_type=F32)
    ba = jnp.dot(ya_ref[...], wba_ref[...], preferred_element_type=F32)
    merged = gates[:, :D_MODEL] * bp + gates[:, D_MODEL:] * ba
    y = jnp.dot(merged.astype(BF16), wout_ref[...], preferred_element_type=F32)
    x1 = _layer_norm(ALPHA * x0_ref[...] + y, l1g_ref[...], l1b_ref[...])
    x1_ref[...] = x1

    x1b = x1.astype(BF16)
    e = (_sigmoid(jnp.dot(x1b, wg_ref[...], preferred_element_type=F32))
         * jnp.dot(p_ref[...].astype(BF16), wp_ref[...], preferred_element_type=F32))
    r_ref[...] = ALPHA * x1 + e


def _merge(hn, ya, x0, p2, wgrp, psc, wbp, wba, gate_b, wout, l1g, l1b, wg, wp, seq):
    t = hn.shape[0]
    tm = TM_MERGE
    assert seq % tm == 0 and tm % HALO == 0
    nblk = seq // tm
    const2 = lambda i: (0, 0)
    single = pl.Buffered(1)
    wspec = lambda shape: pl.BlockSpec(shape, (lambda i: (0,) * len(shape)), pipeline_mode=single)
    return pl.pallas_call(
        functools.partial(_merge_kernel, nblk=nblk),
        grid=(t // tm,),
        in_specs=[pl.BlockSpec((tm, GATE_WIDTH), lambda i: (i, COL_G // GATE_WIDTH)),
                  pl.BlockSpec((tm, POOL_WIDTH), lambda i: (i, COL_P // POOL_WIDTH)),
                  pl.BlockSpec((HALO, POOL_WIDTH),
                               lambda i: (jnp.maximum(i * (tm // HALO) - 1, 0), COL_P // POOL_WIDTH)),
                  pl.BlockSpec((tm, ATTN_WIDTH), lambda i: (i, 0)),
                  pl.BlockSpec((tm, D_MODEL), lambda i: (i, 0)),
                  pl.BlockSpec((tm, PLE_DIM), lambda i: (i, 0)),
                  wspec((POOL_GROUPS, POOL_GROUP_DIM, POOL_GROUP_DIM)),
                  wspec((1, POOL_WIDTH)),
                  wspec((POOL_WIDTH, D_MODEL)),
                  wspec((ATTN_WIDTH, D_MODEL)),
                  wspec((1, GATE_WIDTH)),
                  wspec((D_MODEL, D_MODEL)),
                  wspec((1, D_MODEL)),
                  wspec((1, D_MODEL)),
                  wspec((D_MODEL, D_MODEL)),
                  wspec((PLE_DIM, D_MODEL))],
        out_specs=[pl.BlockSpec((tm, D_MODEL), lambda i: (i, 0)),
                   pl.BlockSpec((tm, D_MODEL), lambda i: (i, 0))],
        out_shape=[jax.ShapeDtypeStruct((t, D_MODEL), F32),
                   jax.ShapeDtypeStruct((t, D_MODEL), F32)],
        compiler_params=pltpu.CompilerParams(
            dimension_semantics=("arbitrary",), vmem_limit_bytes=VMEM_LIMIT_BYTES),
        name="merge",
    )(hn, hn, hn, ya, x0, p2, wgrp, psc, wbp, wba, gate_b, wout, l1g, l1b, wg, wp)


_NTOP = PEER_TOPK + 1
_CAND = [(a, b) for a in range(_NTOP) for b in range(_NTOP)
         if (a + 1) * (b + 1) <= _NTOP]
_CAND_ROWS = -(-len(_CAND) // SUBLANE) * SUBLANE


def _sorting_network(n):
    pairs = []
    p = 1
    while p < n:
        k = p
        while k >= 1:
            for j in range(k % p, n - k, 2 * k):
                for i in range(min(k, n - j - k)):
                    if (i + j) // (2 * p) == (i + j + k) // (2 * p):
                        pairs.append((i + j, i + j + k))
            k //= 2
        p *= 2
    return pairs


def _top_values_sorted(s, k):
    ng = s.shape[0] // SUBLANE
    wires = 1 << (ng - 1).bit_length()
    v = [s[g * SUBLANE:(g + 1) * SUBLANE, :] for g in range(ng)]
    v += [jnp.full_like(v[0], -jnp.inf)] * (wires - ng)
    for i, j in _sorting_network(wires):
        v[i], v[j] = jnp.maximum(v[i], v[j]), jnp.minimum(v[i], v[j])
    v = v[:ng]
    vals = []
    for r in range(k):
        m = jnp.max(v[0], axis=0, keepdims=True)
        vals.append(m)
        left = k - 1 - r
        if left == 0:
            break
        hit = v[0] == m
        for d in range(min(left, ng - 1)):
            v[d] = jnp.where(hit, v[d + 1], v[d])
        if left >= ng:
            v[ng - 1] = jnp.where(hit, -jnp.inf, v[ng - 1])
    return vals


def _peer_kernel(x1_ref, r_ref, wqt_ref, k1_ref, k2_ref, u_ref, vt_ref, g_ref, b_ref, o_ref,
                 xt_scr, e1_scr, e2_scr, kp_scr, cand_scr, h_scr, a_scr, acc_scr, *, nc):
    c = pl.program_id(1)
    tm = x1_ref.shape[0]
    ec = u_ref.shape[0]

    @pl.when(c == 0)
    def _():
        xt = x1_ref[...].T.astype(BF16)
        xt_scr[...] = xt
        cand_scr[...] = jnp.full(cand_scr.shape, -jnp.inf, F32)
        hg = PEER_HEADS // 2
        rows_g = hg * 2 * PEER_HALF
        qts = [jnp.dot(wqt_ref[g * rows_g:(g + 1) * rows_g, :], xt, preferred_element_type=F32)
               for g in range(2)]
        for h in range(PEER_HEADS):
            qt = qts[h // hg]
            base = (h % hg) * 2 * PEER_HALF
            s1 = jnp.dot(k1_ref[...], qt[base:base + PEER_HALF].astype(BF16),
                         preferred_element_type=F32)
            s2 = jnp.dot(k2_ref[...], qt[base + PEER_HALF:base + 2 * PEER_HALF].astype(BF16),
                         preferred_element_type=F32)
            for lt in range(tm // LANE):
                cols = slice(lt * LANE, (lt + 1) * LANE)
                s1t, s2t = s1[:, cols], s2[:, cols]
                v1 = _top_values_sorted(s1t, _NTOP)
                v2 = _top_values_sorted(s2t, _NTOP)
                for n, (a, b) in enumerate(_CAND):
                    cand_scr[n:n + 1, cols] = v1[a] + v2[b]
                cand = cand_scr[:, cols]
                top = _top_values_sorted(cand, _NTOP)
                tau = 0.5 * (top[PEER_TOPK - 1] + top[PEER_TOPK])
                cmax = top[0]
                z = jnp.sum(jnp.where(cand >= tau, jnp.exp(cand - cmax), 0.0), axis=0, keepdims=True)
                logz = cmax + jnp.log(z)
                e1_scr[h, lt] = jnp.exp2((s1t + v2[0] - logz) * LOG2E - 1.0)
                e2_scr[h, :, cols] = jnp.exp2((s2t - v2[0]) * LOG2E)
                kp_scr[h, :, cols] = jnp.exp2((tau - logz) * LOG2E - 1.0)
        acc_scr[...] = jnp.zeros(acc_scr.shape, F32)
        h_scr[0] = jnp.dot(u_ref[...], xt, preferred_element_type=F32)

    slot = c % 2
    nal = ec // N_KEYS
    kb_rows = N_KEYS // 2

    def stage_a():
        h_scr[slot] = jnp.dot(u_ref[...], xt_scr[...], preferred_element_type=F32)

    def stage_b():
        for lt in range(tm // LANE):
            cols = slice(lt * LANE, (lt + 1) * LANE)
            for half in range(N_KEYS // kb_rows):
                rows = slice(half * kb_rows, (half + 1) * kb_rows)
                w = [jnp.zeros((kb_rows, LANE), F32) for _ in range(nal)]
                for h in range(PEER_HEADS):
                    e2t = e2_scr[h, rows, cols]
                    kp = kp_scr[h, :, cols]
                    for al in range(nal):
                        e1 = e1_scr[h, lt, pl.ds((c - 1) * nal + al, 1), :]
                        e = e2t * e1
                        w[al] = w[al] + jnp.where(e >= kp, e, 0.0)
                for al in range(nal):
                    r0 = al * N_KEYS + half * kb_rows
                    hv = h_scr[1 - slot, r0:r0 + kb_rows, cols]
                    a = hv * (1.0 + lax.erf(hv * np.float32(math.sqrt(0.5)))) * w[al]
                    a_scr[1 - slot, r0:r0 + kb_rows, cols] = a.astype(BF16)

    def stage_c():
        acc_scr[...] += jnp.dot(vt_ref[...], a_scr[slot], preferred_element_type=F32)

    @pl.when(jnp.logical_and(c >= 2, c < nc))
    def _():
        stage_c()
        stage_b()
        stage_a()

    @pl.when(c == 1)
    def _():
        stage_b()
        stage_a()

    @pl.when(c == nc)
    def _():
        stage_c()
        stage_b()

    @pl.when(c == nc + 1)
    def _():
        stage_c()


    @pl.when(c == pl.num_programs(1) - 1)
    def _():
        y = r_ref[...] + acc_scr[...].T
        o_ref[...] = _layer_norm(y, g_ref[...], b_ref[...])


def _peer(x1, r, wqt, k1, k2, u, vt, l2g, l2b):
    t = x1.shape[0]
    tm, ec = TM_PEER, EC_PEER
    assert t % tm == 0 and N_EXPERTS % ec == 0 and ec % N_KEYS == 0
    nc = N_EXPERTS // ec
    single = pl.Buffered(1)
    row = lambda i, c: (i, 0)
    const = lambda i, c: (0, 0)
    stat = pltpu.VMEM((PEER_HEADS, N_KEYS, tm), F32)
    return pl.pallas_call(
        functools.partial(_peer_kernel, nc=nc),
        grid=(t // tm, nc + 2),
        in_specs=[pl.BlockSpec((tm, D_MODEL), row, pipeline_mode=single),
                  pl.BlockSpec((tm, D_MODEL), row, pipeline_mode=single),
                  pl.BlockSpec((PEER_HEADS * 2 * PEER_HALF, D_MODEL), const, pipeline_mode=single),
                  pl.BlockSpec((N_KEYS, PEER_HALF), const, pipeline_mode=single),
                  pl.BlockSpec((N_KEYS, PEER_HALF), const, pipeline_mode=single),
                  pl.BlockSpec((ec, D_MODEL), lambda i, c: (jnp.minimum(c, nc - 1), 0)),
                  pl.BlockSpec((None, D_MODEL, ec), lambda i, c: (jnp.clip(c - 2, 0, nc - 1), 0, 0)),
                  pl.BlockSpec((1, D_MODEL), const, pipeline_mode=single),
                  pl.BlockSpec((1, D_MODEL), const, pipeline_mode=single)],
        out_specs=pl.BlockSpec((tm, D_MODEL), row, pipeline_mode=single),
        out_shape=jax.ShapeDtypeStruct((t, D_MODEL), F32),
        scratch_shapes=[pltpu.VMEM((D_MODEL, tm), BF16),
                        pltpu.VMEM((PEER_HEADS, tm // LANE, N_KEYS, LANE), F32), stat,
                        pltpu.VMEM((PEER_HEADS, 1, tm), F32),
                        pltpu.VMEM((_CAND_ROWS, tm), F32),
                        pltpu.VMEM((2, ec, tm), F32), pltpu.VMEM((2, ec, tm), BF16),
                        pltpu.VMEM((D_MODEL, tm), F32)],
        compiler_params=pltpu.CompilerParams(
            dimension_semantics=("arbitrary", "arbitrary"), vmem_limit_bytes=VMEM_LIMIT_BYTES,
        ),
        name="peer",
    )(x1, r, wqt, k1, k2, u, vt, l2g.reshape(1, -1), l2b.reshape(1, -1))


def kernel(x, p, ln_in_g, ln_in_b, w_in, pool_grp_w, pool_scale, lambda_q1, lambda_k1,
           lambda_q2, lambda_k2, subln_g, w_branch_pool, w_branch_attn, gate_b, w_out,
           ln1_g, ln1_b, peer_wq, peer_keys1, peer_keys2, peer_u, peer_v, ple_w, ple_gate_w,
           ln2_g, ln2_b, rel_bias):
    batch, seq, d = x.shape
    assert d == D_MODEL and w_in.shape[0] == DEPTH == 1
    t = batch * seq
    c0, c1 = POOL_WIDTH, POOL_WIDTH + QK_WIDTH
    c2 = c1 + QK_WIDTH
    c3 = c2 + ATTN_WIDTH

    w = w_in[0]
    wq = w[:, c0:c1] * (HEAD_DIM ** -0.5 * LOG2E)
    w_nat = jnp.concatenate([w[:, c3:], w[:, :c0], wq, w[:, c1:c2]], axis=1).astype(BF16)
    w_nat = w_nat.reshape(D_MODEL, NAT_COLS // TN_IN, TN_IN).transpose(1, 0, 2)
    w_vt = w[:, c2:c3].T.astype(BF16)

    bias, lam = _attn_bias(rel_bias, lambda_q1[0], lambda_k1[0], lambda_q2[0], lambda_k2[0])
    hn, x0, vt = _in_proj(x.reshape(t, d), ln_in_g, ln_in_b, w_nat, w_vt, batch, seq)
    ya = _diff_attn(lam, hn, vt, bias, subln_g[0], batch, seq)
    x1, r = _merge(hn, ya, x0, p[0].reshape(t, PLE_DIM),
                   pool_grp_w[0].astype(BF16), pool_scale[0].reshape(1, -1),
                   w_branch_pool[0].astype(BF16), w_branch_attn[0].astype(BF16),
                   gate_b[0].reshape(1, -1), w_out[0].astype(BF16),
                   ln1_g[0].reshape(1, -1), ln1_b[0].reshape(1, -1),
                   ple_gate_w[0].astype(BF16), ple_w[0].astype(BF16), seq)
    out = _peer(x1, r, peer_wq[0].T.astype(BF16), peer_keys1[0].astype(BF16),
                peer_keys2[0].astype(BF16), peer_u[0].astype(BF16),
                peer_v[0].astype(BF16).reshape(N_EXPERTS // EC_PEER, EC_PEER, d).transpose(0, 2, 1),
                ln2_g[0], ln2_b[0])
    return out.reshape(batch, seq, d)
```

```python
import functools
import math

import jax
import jax.numpy as jnp
import numpy as np
from jax import lax
from jax.experimental import pallas as pl
from jax.experimental.pallas import tpu as pltpu

F32 = jnp.float32
BF16 = jnp.bfloat16

D_MODEL = 2048
DEPTH = 1
CHUNK = 64
POOL_WINDOWS = (2, 4, 8, 16)
POOL_GROUPS = 4
POOL_WIDTH = D_MODEL // 2
POOL_GROUP_DIM = POOL_WIDTH // POOL_GROUPS
N_HEADS = 8
HEAD_DIM = D_MODEL // 32
V_DIM = 2 * HEAD_DIM
QK_WIDTH = N_HEADS * 2 * HEAD_DIM
ATTN_WIDTH = N_HEADS * V_DIM
N_BRANCHES = 2
GATE_WIDTH = N_BRANCHES * D_MODEL
NUM_BUCKETS = 32
MAX_DISTANCE = 128
PEER_HEADS = 8
N_KEYS = 128
N_EXPERTS = N_KEYS * N_KEYS
PEER_HALF = 128
PEER_TOPK = 16
PLE_DIM = 256
ALPHA = (2 * DEPTH) ** 0.25
LN_EPS = 1e-5
MASK_VALUE = -1e30
LAM_INIT = 0.8 - 0.6 * math.exp(-0.3 * 0)

LANE = 128
SUBLANE = 8
VMEM_LIMIT_BYTES = 60 * 1024 * 1024

V_AUG = V_DIM + 16
LOG2E = math.log2(math.e)
TQ = 256
TKV = 256
ATTN_UNROLL = 8
HALO = 16
TM_IN = 512
TN_IN = 1792
TM_MERGE = 256
TM_PEER = 512
EC_PEER = 512

COL_G = 0
COL_P = GATE_WIDTH
COL_Q = COL_P + POOL_WIDTH
COL_K = COL_Q + QK_WIDTH
NAT_COLS = COL_K + QK_WIDTH


def _layer_norm(x, g, b):
    mu = jnp.mean(x, axis=-1, keepdims=True)
    xc = x - mu
    var = jnp.mean(xc * xc, axis=-1, keepdims=True)
    return xc * lax.rsqrt(var + LN_EPS) * g + b


def _sigmoid(x):
    return 1.0 / (1.0 + jnp.exp(-x))


def _t5_bucket(rel):
    nb = NUM_BUCKETS // 2
    ret = (rel > 0).astype(jnp.int32) * nb
    n = jnp.abs(rel)
    max_exact = nb // 2
    is_small = n < max_exact
    nf = jnp.maximum(n, 1).astype(F32)
    large = max_exact + (jnp.log(nf / max_exact) / math.log(MAX_DISTANCE / max_exact)
                         * (nb - max_exact)).astype(jnp.int32)
    large = jnp.minimum(large, nb - 1)
    return ret + jnp.where(is_small, n, large)


def _bias_kernel(tab_ref, lq1_ref, lk1_ref, lq2_ref, lk2_ref, bias_ref, lam_ref):
    h = pl.program_id(0)
    shape = (2 * TKV, 2 * TQ)
    c = lax.broadcasted_iota(jnp.int32, shape, 0)
    r = lax.broadcasted_iota(jnp.int32, shape, 1)
    r = jnp.where(r >= TQ, r - TQ, r)
    rel = c - TKV - r
    bucket = _t5_bucket(rel)
    far = tab_ref[h, NUM_BUCKETS // 2 - 1]
    val = jnp.zeros(shape, F32)
    for b in range(NUM_BUCKETS):
        val = jnp.where(bucket == b, (tab_ref[h, b] - far) * LOG2E, val)
    shift = CHUNK.bit_length() - 1
    allowed = lax.shift_right_logical(c, shift) <= lax.shift_right_logical(r, shift) + TKV // CHUNK
    bias_ref[0:TKV, :] = jnp.zeros((TKV, 2 * TQ), F32)
    bias_ref[TKV:3 * TKV, :] = jnp.where(allowed, val, MASK_VALUE)
    lam = (jnp.exp(jnp.sum(lq1_ref[...] * lk1_ref[...], axis=-1, keepdims=True))
           - jnp.exp(jnp.sum(lq2_ref[...] * lk2_ref[...], axis=-1, keepdims=True)) + LAM_INIT)
    lam_ref[...] = jnp.broadcast_to(lam, lam_ref.shape)


def _attn_bias(rel_bias, lq1, lk1, lq2, lk2):
    assert TQ == TKV and TQ > MAX_DISTANCE and TQ % CHUNK == 0
    tab = rel_bias.T.astype(F32)
    vec = pl.BlockSpec((1, HEAD_DIM), lambda h: (0, 0))
    return pl.pallas_call(
        _bias_kernel,
        grid=(N_HEADS,),
        in_specs=[pl.BlockSpec(memory_space=pltpu.SMEM), vec, vec, vec, vec],
        out_specs=[pl.BlockSpec((None, 3 * TKV, 2 * TQ), lambda h: (h, 0, 0)),
                   pl.BlockSpec((None, SUBLANE, LANE), lambda h: (h, 0, 0))],
        out_shape=[jax.ShapeDtypeStruct((N_HEADS, 3 * TKV, 2 * TQ), F32),
                   jax.ShapeDtypeStruct((N_HEADS, SUBLANE, LANE), F32)],
        name="attn_bias",
    )(tab, lq1.reshape(1, -1), lk1.reshape(1, -1), lq2.reshape(1, -1), lk2.reshape(1, -1))


def _inproj_kernel(x_ref, g_ref, b_ref, w_ref, wvt_ref, hn_ref, x0_ref, vt_ref, xb_scr):
    j = pl.program_id(1)

    @pl.when(j == 0)
    def _():
        y = _layer_norm(x_ref[...], g_ref[...], b_ref[...])
        x0_ref[...] = y
        yb = y.astype(BF16)
        xb_scr[...] = yb
        vt = lax.dot_general(wvt_ref[...], yb, (((1,), (1,)), ((), ())),
                             preferred_element_type=F32)
        row = lax.broadcasted_iota(jnp.int32, (V_AUG - V_DIM, TKV), 0)
        ones_row = jnp.where(row == 0, 1.0, 0.0).astype(BF16)
        for s in range(vt_ref.shape[0]):
            for h in range(N_HEADS):
                vt_ref[s, h, 0:V_DIM, :] = vt[h * V_DIM:(h + 1) * V_DIM,
                                              s * TKV:(s + 1) * TKV].astype(BF16)
                vt_ref[s, h, V_DIM:V_AUG, :] = ones_row

    hn_ref[...] = jnp.dot(xb_scr[...], w_ref[...], preferred_element_type=F32).astype(BF16)


def _in_proj(x2, ln_g, ln_b, w_nat, w_vt, batch, seq):
    t = x2.shape[0]
    tm, tn = TM_IN, TN_IN
    assert seq % tm == 0 and tm % TKV == 0 and NAT_COLS % tn == 0
    nblk = seq // tm
    const = lambda i, j: (0, 0)
    return pl.pallas_call(
        _inproj_kernel,
        grid=(t // tm, NAT_COLS // tn),
        in_specs=[pl.BlockSpec((tm, D_MODEL), lambda i, j: (i, 0)),
                  pl.BlockSpec((1, D_MODEL), const),
                  pl.BlockSpec((1, D_MODEL), const),
                  pl.BlockSpec((None, D_MODEL, tn), lambda i, j: (j, 0, 0)),
                  pl.BlockSpec((ATTN_WIDTH, D_MODEL), const)],
        out_specs=[pl.BlockSpec((tm, tn), lambda i, j: (i, j)),
                   pl.BlockSpec((tm, D_MODEL), lambda i, j: (i, 0)),
                   pl.BlockSpec((None, tm // TKV, N_HEADS, V_AUG, TKV),
                                lambda i, j: (i // nblk, i % nblk, 0, 0, 0))],
        out_shape=[jax.ShapeDtypeStruct((t, NAT_COLS), BF16),
                   jax.ShapeDtypeStruct((t, D_MODEL), F32),
                   jax.ShapeDtypeStruct((batch, seq // TKV, N_HEADS, V_AUG, TKV), BF16)],
        scratch_shapes=[pltpu.VMEM((tm, D_MODEL), BF16)],
        compiler_params=pltpu.CompilerParams(
            dimension_semantics=("arbitrary", "arbitrary"), vmem_limit_bytes=VMEM_LIMIT_BYTES),
        name="in_proj",
    )(x2, ln_g.reshape(1, -1), ln_b.reshape(1, -1), w_nat, w_vt)


def _attn_kernel(lam_ref, q_ref, k_ref, vt_ref, bias_ref, g_ref, o_ref,
                 q2_scr, s_scr, p_scr, al_scr, m_scr, acc_scr, *, nq):
    npairs = nq * (nq + 1) // 2
    lane = lax.broadcasted_iota(jnp.int32, (TQ, LANE), 1)

    def prep(i, carry):
        qs = q_ref[pl.ds(pl.multiple_of(i * TQ, TQ), TQ), :].astype(F32)
        q2_scr[i] = jnp.concatenate([jnp.where(lane < HEAD_DIM, qs, 0.0),
                                     jnp.where(lane >= HEAD_DIM, qs, 0.0)], axis=0).astype(BF16)
        m_scr[i] = jnp.full(m_scr.shape[1:], -jnp.inf, F32)
        acc_scr[i] = jnp.zeros(acc_scr.shape[1:], F32)
        return carry

    lax.fori_loop(0, nq, prep, 0)
    s_scr[1] = jnp.zeros(s_scr.shape[1:], F32)
    p_scr[0] = jnp.zeros(p_scr.shape[1:], BF16)
    al_scr[0] = jnp.ones(al_scr.shape[1:], F32)

    def advance(i, j):
        wrap = j >= i
        last = jnp.logical_and(i == nq - 1, wrap)
        ni = jnp.where(jnp.logical_and(wrap, jnp.logical_not(last)), i + 1, i)
        nj = jnp.where(last, j, jnp.where(wrap, 0, j + 1))
        return ni, nj

    def substep(t, sa, pairs):
        (ia, ja), (ib, jb), (ic, jc) = pairs
        sb = 1 - sa
        pv = jnp.dot(vt_ref[jc], p_scr[sa], preferred_element_type=F32)
        acc_scr[ic] = acc_scr[ic] * al_scr[sa] + pv
        live_b = jnp.logical_and(t >= 1, t <= npairs)
        st = s_scr[sb]
        m_prev = m_scr[ib]
        m_new = jnp.where(live_b, jnp.maximum(m_prev, jnp.max(st, axis=0, keepdims=True)), m_prev)
        al_scr[sb] = jnp.where(live_b, jnp.exp2(m_prev - m_new), 1.0)
        m_scr[ib] = m_new
        p_scr[sb] = jnp.exp2(st - jnp.where(live_b, m_new, jnp.inf)).astype(BF16)
        kblk = k_ref[pl.ds(pl.multiple_of(ja * TKV, TKV), TKV), :]
        boff = jnp.clip(ja - (ia - 2), 0, 2) * TKV
        s_scr[sa] = lax.dot_general(kblk, q2_scr[ia], (((1,), (1,)), ((), ())),
                                    preferred_element_type=F32) + bias_ref[pl.ds(pl.multiple_of(boff, TKV), TKV), :]
        return (advance(ia, ja), (ia, ja), (ib, jb))

    def body(tt, pairs):
        for u in range(ATTN_UNROLL):
            pairs = substep(ATTN_UNROLL * tt + u, u % 2, pairs)
        return pairs

    zero = jnp.int32(0)
    lax.fori_loop(0, pl.cdiv(npairs + 2, ATTN_UNROLL), body, ((zero, zero), (zero, zero), (zero, zero)))

    lam = lam_ref[0:1, 0:1]

    def finish(i, carry):
        acc = acc_scr[i]
        o = acc[0:V_DIM, :] / acc[V_DIM:V_DIM + 1, :]
        att = o[:, :TQ] - lam * o[:, TQ:]
        ms = jnp.mean(att * att, axis=0, keepdims=True)
        y = att * lax.rsqrt(ms + LN_EPS) * g_ref[...] * (1.0 - LAM_INIT)
        o_ref[pl.ds(pl.multiple_of(i * TQ, TQ), TQ), :] = y.T.astype(BF16)
        return carry

    lax.fori_loop(0, nq, finish, 0)


def _diff_attn(lam, hn, vt, bias, subln_g, batch, seq):
    t = hn.shape[0]
    nq = seq // TQ
    qb, kb = COL_Q // LANE, COL_K // LANE
    return pl.pallas_call(
        functools.partial(_attn_kernel, nq=nq),
        grid=(batch, N_HEADS),
        in_specs=[pl.BlockSpec((None, SUBLANE, LANE), lambda b, h: (h, 0, 0)),
                  pl.BlockSpec((seq, LANE), lambda b, h: (b, qb + h)),
                  pl.BlockSpec((seq, LANE), lambda b, h: (b, kb + h)),
                  pl.BlockSpec((None, seq // TKV, None, V_AUG, TKV), lambda b, h: (b, 0, h, 0, 0)),
                  pl.BlockSpec((None, 3 * TKV, 2 * TQ), lambda b, h: (h, 0, 0)),
                  pl.BlockSpec((V_DIM, 1), lambda b, h: (0, 0))],
        out_specs=pl.BlockSpec((seq, V_DIM), lambda b, h: (b, h)),
        out_shape=jax.ShapeDtypeStruct((t, ATTN_WIDTH), BF16),
        scratch_shapes=[pltpu.VMEM((nq, 2 * TQ, LANE), BF16),
                        pltpu.VMEM((2, TKV, 2 * TQ), F32), pltpu.VMEM((2, TKV, 2 * TQ), BF16),
                        pltpu.VMEM((2, 1, 2 * TQ), F32),
                        pltpu.VMEM((nq, 1, 2 * TQ), F32),
                        pltpu.VMEM((nq, V_AUG, 2 * TQ), F32)],
        compiler_params=pltpu.CompilerParams(
            dimension_semantics=("arbitrary", "arbitrary"),
            vmem_limit_bytes=VMEM_LIMIT_BYTES),
        name="diff_attn",
    )(lam, hn, hn, vt, bias, subln_g.reshape(-1, 1))


def _merge_kernel(gp_ref, xp_ref, halo_ref, ya_ref, x0_ref, p_ref, wgrp_ref, psc_ref, wbp_ref,
                  wba_ref, gb_ref, wout_ref, l1g_ref, l1b_ref, wg_ref, wp_ref,
                  x1_ref, r_ref, *, nblk):
    i = pl.program_id(0)
    tm = xp_ref.shape[0]
    blk_in_seq = i % nblk
    xp = xp_ref[...].astype(F32)
    halo = halo_ref[...].astype(F32)
    halo = jnp.where(blk_in_seq == 0, jnp.zeros_like(halo), halo)
    ext = jnp.concatenate([halo, xp], axis=0)
    pos = blk_in_seq * tm + lax.broadcasted_iota(jnp.int32, (tm, 1), 0)

    yps = []
    for gi, w in enumerate(POOL_WINDOWS):
        lo, hi = gi * POOL_GROUP_DIM, (gi + 1) * POOL_GROUP_DIM
        s = ext[:, lo:hi]
        span = 1
        while span < w:
            s = s[span:] + s[:-span]
            span *= 2
        start = HALO - (w - 1)
        win = s[start:start + tm]
        cnt = jnp.minimum(pos + 1, w).astype(F32)
        mixed = win / cnt - xp[:, lo:hi]
        yps.append(jnp.dot(mixed.astype(BF16), wgrp_ref[gi], preferred_element_type=F32))
    yp = jnp.concatenate(yps, axis=1) * psc_ref[...]

    gates = _sigmoid(gp_ref[...].astype(F32) + gb_ref[...])
    bp = jnp.dot(yp.astype(BF16), wbp_ref[...], preferred_element_type=F32)
    ba = jnp.dot(ya_ref[...], wba_ref[...], preferred_element_type=F32)
    merged = gates[:, :D_MODEL] * bp + gates[:, D_MODEL:] * ba
    y = jnp.dot(merged.astype(BF16), wout_ref[...], preferred_element_type=F32)
    x1 = _layer_norm(ALPHA * x0_ref[...] + y, l1g_ref[...], l1b_ref[...])
    x1_ref[...] = x1

    x1b = x1.astype(BF16)
    e = (_sigmoid(jnp.dot(x1b, wg_ref[...], preferred_element_type=F32))
         * jnp.dot(p_ref[...].astype(BF16), wp_ref[...], preferred_element_type=F32))
    r_ref[...] = ALPHA * x1 + e


def _merge(hn, ya, x0, p2, wgrp, psc, wbp, wba, gate_b, wout, l1g, l1b, wg, wp, seq):
    t = hn.shape[0]
    tm = TM_MERGE
    assert seq % tm == 0 and tm % HALO == 0
    nblk = seq // tm
    const2 = lambda i: (0, 0)
    single = pl.Buffered(1)
    wspec = lambda shape: pl.BlockSpec(shape, (lambda i: (0,) * len(shape)), pipeline_mode=single)
    return pl.pallas_call(
        functools.partial(_merge_kernel, nblk=nblk),
        grid=(t // tm,),
        in_specs=[pl.BlockSpec((tm, GATE_WIDTH), lambda i: (i, COL_G // GATE_WIDTH)),
                  pl.BlockSpec((tm, POOL_WIDTH), lambda i: (i, COL_P // POOL_WIDTH)),
                  pl.BlockSpec((HALO, POOL_WIDTH),
                               lambda i: (jnp.maximum(i * (tm // HALO) - 1, 0), COL_P // POOL_WIDTH)),
                  pl.BlockSpec((tm, ATTN_WIDTH), lambda i: (i, 0)),
                  pl.BlockSpec((tm, D_MODEL), lambda i: (i, 0)),
                  pl.BlockSpec((tm, PLE_DIM), lambda i: (i, 0)),
                  wspec((POOL_GROUPS, POOL_GROUP_DIM, POOL_GROUP_DIM)),
                  wspec((1, POOL_WIDTH)),
                  wspec((POOL_WIDTH, D_MODEL)),
                  wspec((ATTN_WIDTH, D_MODEL)),
                  wspec((1, GATE_WIDTH)),
                  wspec((D_MODEL, D_MODEL)),
                  wspec((1, D_MODEL)),
                  wspec((1, D_MODEL)),
                  wspec((D_MODEL, D_MODEL)),
                  wspec((PLE_DIM, D_MODEL))],
        out_specs=[pl.BlockSpec((tm, D_MODEL), lambda i: (i, 0)),
                   pl.BlockSpec((tm, D_MODEL), lambda i: (i, 0))],
        out_shape=[jax.ShapeDtypeStruct((t, D_MODEL), F32),
                   jax.ShapeDtypeStruct((t, D_MODEL), F32)],
        compiler_params=pltpu.CompilerParams(
            dimension_semantics=("arbitrary",), vmem_limit_bytes=VMEM_LIMIT_BYTES),
        name="merge",
    )(hn, hn, hn, ya, x0, p2, wgrp, psc, wbp, wba, gate_b, wout, l1g, l1b, wg, wp)


_NTOP = PEER_TOPK + 1
_CAND = [(a, b) for a in range(_NTOP) for b in range(_NTOP)
         if (a + 1) * (b + 1) <= _NTOP]
_CAND_ROWS = -(-len(_CAND) // SUBLANE) * SUBLANE


def _sorting_network(n):
    pairs = []
    p = 1
    while p < n:
        k = p
        while k >= 1:
            for j in range(k % p, n - k, 2 * k):
                for i in range(min(k, n - j - k)):
                    if (i + j) // (2 * p) == (i + j + k) // (2 * p):
                        pairs.append((i + j, i + j + k))
            k //= 2
        p *= 2
    return pairs


def _top_values_sorted(s, k):
    ng = s.shape[0] // SUBLANE
    wires = 1 << (ng - 1).bit_length()
    v = [s[g * SUBLANE:(g + 1) * SUBLANE, :] for g in range(ng)]
    v += [jnp.full_like(v[0], -jnp.inf)] * (wires - ng)
    for i, j in _sorting_network(wires):
        v[i], v[j] = jnp.maximum(v[i], v[j]), jnp.minimum(v[i], v[j])
    v = v[:ng]
    vals = []
    for r in range(k):
        m = jnp.max(v[0], axis=0, keepdims=True)
        vals.append(m)
        left = k - 1 - r
        if left == 0:
            break
        hit = v[0] == m
        for d in range(min(left, ng - 1)):
            v[d] = jnp.where(hit, v[d + 1], v[d])
        if left >= ng:
            v[ng - 1] = jnp.where(hit, -jnp.inf, v[ng - 1])
    return vals


def _peer_kernel(x1_ref, r_ref, wqt_ref, k1_ref, k2_ref, u_ref, vt_ref, g_ref, b_ref, o_ref,
                 xt_scr, e1_scr, e2_scr, kp_scr, cand_scr, h_scr, a_scr, acc_scr, *, nc):
    c = pl.program_id(1)
    tm = x1_ref.shape[0]
    ec = u_ref.shape[0]

    @pl.when(c == 0)
    def _():
        xt = x1_ref[...].T.astype(BF16)
        xt_scr[...] = xt
        cand_scr[...] = jnp.full(cand_scr.shape, -jnp.inf, F32)
        hg = PEER_HEADS // 2
        rows_g = hg * 2 * PEER_HALF
        qts = [jnp.dot(wqt_ref[g * rows_g:(g + 1) * rows_g, :], xt, preferred_element_type=F32)
               for g in range(2)]
        for h in range(PEER_HEADS):
            qt = qts[h // hg]
            base = (h % hg) * 2 * PEER_HALF
            s1 = jnp.dot(k1_ref[...], qt[base:base + PEER_HALF].astype(BF16),
                         preferred_element_type=F32)
            s2 = jnp.dot(k2_ref[...], qt[base + PEER_HALF:base + 2 * PEER_HALF].astype(BF16),
                         preferred_element_type=F32)
            for lt in range(tm // LANE):
                cols = slice(lt * LANE, (lt + 1) * LANE)
                s1t, s2t = s1[:, cols], s2[:, cols]
                v1 = _top_values_sorted(s1t, _NTOP)
                v2 = _top_values_sorted(s2t, _NTOP)
                for n, (a, b) in enumerate(_CAND):
                    cand_scr[n:n + 1, cols] = v1[a] + v2[b]
                cand = cand_scr[:, cols]
                top = _top_values_sorted(cand, _NTOP)
                tau = 0.5 * (top[PEER_TOPK - 1] + top[PEER_TOPK])
                cmax = top[0]
                z = jnp.sum(jnp.where(cand >= tau, jnp.exp(cand - cmax), 0.0), axis=0, keepdims=True)
                logz = cmax + jnp.log(z)
                e1_scr[h, lt] = jnp.exp2((s1t + v2[0] - logz) * LOG2E - 1.0)
                e2_scr[h, :, cols] = jnp.exp2((s2t - v2[0]) * LOG2E)
                kp_scr[h, :, cols] = jnp.exp2((tau - logz) * LOG2E - 1.0)
        acc_scr[...] = jnp.zeros(acc_scr.shape, F32)
        h_scr[0] = jnp.dot(u_ref[...], xt, preferred_element_type=F32)

    slot = c % 2
    nal = ec // N_KEYS
    kb_rows = N_KEYS // 2

    def stage_a():
        h_scr[slot] = jnp.dot(u_ref[...], xt_scr[...], preferred_element_type=F32)

    def stage_b():
        for lt in range(tm // LANE):
            cols = slice(lt * LANE, (lt + 1) * LANE)
            for half in range(N_KEYS // kb_rows):
                rows = slice(half * kb_rows, (half + 1) * kb_rows)
                w = [jnp.zeros((kb_rows, LANE), F32) for _ in range(nal)]
                for h in range(PEER_HEADS):
                    e2t = e2_scr[h, rows, cols]
                    kp = kp_scr[h, :, cols]
                    for al in range(nal):
                        e1 = e1_scr[h, lt, pl.ds((c - 1) * nal + al, 1), :]
                        e = e2t * e1
                        w[al] = w[al] + jnp.where(e >= kp, e, 0.0)
                for al in range(nal):
                    r0 = al * N_KEYS + half * kb_rows
                    hv = h_scr[1 - slot, r0:r0 + kb_rows, cols]
                    a = hv * (1.0 + lax.erf(hv * np.float32(math.sqrt(0.5)))) * w[al]
                    a_scr[1 - slot, r0:r0 + kb_rows, cols] = a.astype(BF16)

    def stage_c():
        acc_scr[...] += jnp.dot(vt_ref[...], a_scr[slot], preferred_element_type=F32)

    @pl.when(jnp.logical_and(c >= 2, c < nc))
    def _():
        stage_c()
        stage_b()
        stage_a()

    @pl.when(c == 1)
    def _():
        stage_b()
        stage_a()

    @pl.when(c == nc)
    def _():
        stage_c()
        stage_b()

    @pl.when(c == nc + 1)
    def _():
        stage_c()


    @pl.when(c == pl.num_programs(1) - 1)
    def _():
        y = r_ref[...] + acc_scr[...].T
        o_ref[...] = _layer_norm(y, g_ref[...], b_ref[...])


def _peer(x1, r, wqt, k1, k2, u, vt, l2g, l2b):
    t = x1.shape[0]
    tm, ec = TM_PEER, EC_PEER
    assert t % tm == 0 and N_EXPERTS % ec == 0 and ec % N_KEYS == 0
    nc = N_EXPERTS // ec
    single = pl.Buffered(1)
    row = lambda i, c: (i, 0)
    const = lambda i, c: (0, 0)
    stat = pltpu.VMEM((PEER_HEADS, N_KEYS, tm), F32)
    return pl.pallas_call(
        functools.partial(_peer_kernel, nc=nc),
        grid=(t // tm, nc + 2),
        in_specs=[pl.BlockSpec((tm, D_MODEL), row, pipeline_mode=single),
                  pl.BlockSpec((tm, D_MODEL), row, pipeline_mode=single),
                  pl.BlockSpec((PEER_HEADS * 2 * PEER_HALF, D_MODEL), const, pipeline_mode=single),
                  pl.BlockSpec((N_KEYS, PEER_HALF), const, pipeline_mode=single),
                  pl.BlockSpec((N_KEYS, PEER_HALF), const, pipeline_mode=single),
                  pl.BlockSpec((ec, D_MODEL), lambda i, c: (jnp.minimum(c, nc - 1), 0)),
                  pl.BlockSpec((None, D_MODEL, ec), lambda i, c: (jnp.clip(c - 2, 0, nc - 1), 0, 0)),
                  pl.BlockSpec((1, D_MODEL), const, pipeline_mode=single),
                  pl.BlockSpec((1, D_MODEL), const, pipeline_mode=single)],
        out_specs=pl.BlockSpec((tm, D_MODEL), row, pipeline_mode=single),
        out_shape=jax.ShapeDtypeStruct((t, D_MODEL), F32),
        scratch_shapes=[pltpu.VMEM((D_MODEL, tm), BF16),
                        pltpu.VMEM((PEER_HEADS, tm // LANE, N_KEYS, LANE), F32), stat,
                        pltpu.VMEM((PEER_HEADS, 1, tm), F32),
                        pltpu.VMEM((_CAND_ROWS, tm), F32),
                        pltpu.VMEM((2, ec, tm), F32), pltpu.VMEM((2, ec, tm), BF16),
                        pltpu.VMEM((D_MODEL, tm), F32)],
        compiler_params=pltpu.CompilerParams(
            dimension_semantics=("arbitrary", "arbitrary"), vmem_limit_bytes=VMEM_LIMIT_BYTES,
        ),
        name="peer",
    )(x1, r, wqt, k1, k2, u, vt, l2g.reshape(1, -1), l2b.reshape(1, -1))


def kernel(x, p, ln_in_g, ln_in_b, w_in, pool_grp_w, pool_scale, lambda_q1, lambda_k1,
           lambda_q2, lambda_k2, subln_g, w_branch_pool, w_branch_attn, gate_b, w_out,
           ln1_g, ln1_b, peer_wq, peer_keys1, peer_keys2, peer_u, peer_v, ple_w, ple_gate_w,
           ln2_g, ln2_b, rel_bias):
    batch, seq, d = x.shape
    assert d == D_MODEL and w_in.shape[0] == DEPTH == 1
    t = batch * seq
    c0, c1 = POOL_WIDTH, POOL_WIDTH + QK_WIDTH
    c2 = c1 + QK_WIDTH
    c3 = c2 + ATTN_WIDTH

    w = w_in[0]
    wq = w[:, c0:c1] * (HEAD_DIM ** -0.5 * LOG2E)
    w_nat = jnp.concatenate([w[:, c3:], w[:, :c0], wq, w[:, c1:c2]], axis=1).astype(BF16)
    w_nat = w_nat.reshape(D_MODEL, NAT_COLS // TN_IN, TN_IN).transpose(1, 0, 2)
    w_vt = w[:, c2:c3].T.astype(BF16)

    bias, lam = _attn_bias(rel_bias, lambda_q1[0], lambda_k1[0], lambda_q2[0], lambda_k2[0])
    hn, x0, vt = _in_proj(x.reshape(t, d), ln_in_g, ln_in_b, w_nat, w_vt, batch, seq)
    ya = _diff_attn(lam, hn, vt, bias, subln_g[0], batch, seq)
    x1, r = _merge(hn, ya, x0, p[0].reshape(t, PLE_DIM),
                   pool_grp_w[0].astype(BF16), pool_scale[0].reshape(1, -1),
                   w_branch_pool[0].astype(BF16), w_branch_attn[0].astype(BF16),
                   gate_b[0].reshape(1, -1), w_out[0].astype(BF16),
                   ln1_g[0].reshape(1, -1), ln1_b[0].reshape(1, -1),
                   ple_gate_w[0].astype(BF16), ple_w[0].astype(BF16), seq)
    out = _peer(x1, r, peer_wq[0].T.astype(BF16), peer_keys1[0].astype(BF16),
                peer_keys2[0].astype(BF16), peer_u[0].astype(BF16),
                peer_v[0].astype(BF16).reshape(N_EXPERTS // EC_PEER, EC_PEER, d).transpose(0, 2, 1),
                ln2_g[0], ln2_b[0])
    return out.reshape(batch, seq, d)
```
